```python
import math
import jax
import jax.numpy as jnp
from jax import lax
import numpy as np

D_MODEL = 1024
BATCH = 16
SEQ = 2048
DEPTH = 2

ROPE_THETA = 500000.0
NORM_EPS = 1e-6
Q_BLOCK = 128

MLA_HEADS = 8
MLA_NOPE = 64
MLA_ROPE = 32
MLA_V = 64
MLA_Q_LORA = 256
MLA_KV_LORA = 128
CONV_DIM = 512
CONV_WIDTH = 3
DSA_HEADS = 8
DSA_HEAD_DIM = 64
DSA_ROT = DSA_HEAD_DIM // 4
IDX_HEADS = 8
IDX_DIM = 64
IDX_ROT = IDX_DIM // 4
TOPK_MAX = 256
N_BRANCH = 3
BRANCH_WIDTH = 512
IN_SIZES = (MLA_Q_LORA, MLA_KV_LORA + MLA_ROPE, 3 * CONV_DIM, DSA_HEADS * DSA_HEAD_DIM,
            2 * DSA_HEAD_DIM, IDX_HEADS * IDX_DIM, IDX_DIM, IDX_HEADS, N_BRANCH * D_MODEL)
IN_COLS = sum(IN_SIZES)
N_EXPERTS = 32
TOP_K = 4
D_EXPERT = D_MODEL
SWIGLU_ALPHA = 1.702
SWIGLU_LIMIT = 7.0
MOE_BLOCK = 512

kernel_name = "hybrid_mla_conv_dsa_moe_adaln"


def rmsnorm(x, g):
    xf = x.astype(jnp.float32)
    y = xf * lax.rsqrt(jnp.mean(xf * xf, axis=-1, keepdims=True) + NORM_EPS)
    return (y * g.astype(jnp.float32)).astype(x.dtype)


def rope(x, pos, n_rot):
    half = n_rot // 2
    inv_freq = jnp.exp(-math.log(ROPE_THETA) * jnp.arange(half, dtype=jnp.float32) * (2.0 / n_rot))
    ang = pos.astype(jnp.float32)[..., None] * inv_freq
    if x.ndim == 4:
        ang = ang[:, :, None, :]
    cos = jnp.cos(ang).astype(x.dtype)
    sin = jnp.sin(ang).astype(x.dtype)
    x1 = x[..., :half]
    x2 = x[..., half:n_rot]
    return jnp.concatenate([x1 * cos - x2 * sin, x2 * cos + x1 * sin, x[..., n_rot:]], axis=-1)


def split_cols(a, sizes):
    out, start = [], 0
    for n in sizes:
        out.append(a[..., start:start + n])
        start += n
    return out


def to_blocks(a, nb):
    return a.reshape((a.shape[0], nb, Q_BLOCK) + a.shape[2:]).swapaxes(0, 1)


def from_blocks(a):
    a = a.swapaxes(0, 1)
    return a.reshape((a.shape[0], a.shape[1] * a.shape[2]) + a.shape[3:])


def blocked_causal_attention(q, k, v, scale):
    S = q.shape[1]
    nb = S // Q_BLOCK
    kpos = jnp.arange(S)

    def one(args):
        q_b, start = args
        qpos = start + jnp.arange(Q_BLOCK)
        s = jnp.einsum('bqhd,bkhd->bhqk', q_b, k).astype(jnp.float32) * scale
        s = jnp.where((kpos[None, :] <= qpos[:, None])[None, None], s, -jnp.inf)
        p = jax.nn.softmax(s, axis=-1).astype(v.dtype)
        return jnp.einsum('bhqk,bkhd->bqhd', p, v)

    out = lax.map(one, (to_blocks(q, nb), jnp.arange(nb) * Q_BLOCK))
    return from_blocks(out)


def indexer_sparse_attention(q, k, v, qi, ki, wi):
    S = q.shape[1]
    nb = S // Q_BLOCK
    topk = min(TOPK_MAX, S // 4)
    kpos = jnp.arange(S)
    gather = jax.vmap(lambda kb, ib: kb[ib])
    scale = DSA_HEAD_DIM ** -0.5

    def one(args):
        q_b, qi_b, wi_b, start = args
        qpos = start + jnp.arange(Q_BLOCK)
        logits = jnp.einsum('bqhd,bkd->bqhk', qi_b, ki).astype(jnp.float32) * (IDX_DIM ** -0.5)
        score = jnp.einsum('bqhk,bqh->bqk', jax.nn.relu(logits), wi_b.astype(jnp.float32)) * (IDX_HEADS ** -0.5)
        score = jnp.where((kpos[None, :] <= qpos[:, None])[None], score, -jnp.inf)
        _, sel = lax.top_k(score, topk)
        valid = sel <= qpos[None, :, None]
        ks = gather(k, sel)
        vs = gather(v, sel)
        s = jnp.einsum('bqhd,bqkd->bqhk', q_b, ks).astype(jnp.float32) * scale
        s = jnp.where(valid[:, :, None, :], s, -jnp.inf)
        p = jax.nn.softmax(s, axis=-1).astype(v.dtype)
        return jnp.einsum('bqhk,bqkd->bqhd', p, vs)

    out = lax.map(one, (to_blocks(q, nb), to_blocks(qi, nb), to_blocks(wi, nb), jnp.arange(nb) * Q_BLOCK))
    return from_blocks(out)


def causal_depthwise_conv(u, w):
    return lax.conv_general_dilated(u, w[:, None, :].astype(u.dtype), window_strides=(1,),
                                    padding=[(CONV_WIDTH - 1, 0)],
                                    dimension_numbers=('NWC', 'WIO', 'NWC'),
                                    feature_group_count=u.shape[-1])


def token_mixer(h, pos, w_in, b_gate, q_norm_g, w_uq, kv_norm_g, w_ukv, conv_w, w_branch, w_out):
    B, S, D = h.shape
    proj = h @ w_in
    q_lat, kv_lat, conv_in, dq, dkv, iq, ik, iw, gate_logits = split_cols(proj, IN_SIZES)

    cq = rmsnorm(q_lat, q_norm_g)
    qa = rope((cq @ w_uq).reshape(B, S, MLA_HEADS, MLA_ROPE + MLA_NOPE), pos, MLA_ROPE)
    ckv = rmsnorm(kv_lat[..., :MLA_KV_LORA], kv_norm_g)
    k_pe = rope(kv_lat[..., MLA_KV_LORA:], pos, MLA_ROPE)
    kv = (ckv @ w_ukv).reshape(B, S, MLA_HEADS, MLA_NOPE + MLA_V)
    ka = jnp.concatenate([jnp.broadcast_to(k_pe[:, :, None, :], (B, S, MLA_HEADS, MLA_ROPE)),
                          kv[..., :MLA_NOPE]], axis=-1)
    o_a = blocked_causal_attention(qa, ka, kv[..., MLA_NOPE:], (MLA_NOPE + MLA_ROPE) ** -0.5)
    o_a = o_a.reshape(B, S, BRANCH_WIDTH)

    g_b, g_c, hc = split_cols(conv_in, (CONV_DIM, CONV_DIM, CONV_DIM))
    o_b = g_b * causal_depthwise_conv(g_c * hc, conv_w)

    qc = rope(dq.reshape(B, S, DSA_HEADS, DSA_HEAD_DIM), pos, DSA_ROT)
    kc = rope(dkv[..., :DSA_HEAD_DIM], pos, DSA_ROT)
    vc = dkv[..., DSA_HEAD_DIM:]
    qi = rope(iq.reshape(B, S, IDX_HEADS, IDX_DIM), pos, IDX_ROT)
    ki = rope(ik, pos, IDX_ROT)
    o_c = indexer_sparse_attention(qc, kc, vc, qi, ki, iw).reshape(B, S, BRANCH_WIDTH)

    o = jnp.stack([o_a, o_b, o_c], axis=2)
    y = jnp.einsum('bsnw,nwd->bsnd', o, w_branch)
    g = jax.nn.sigmoid(gate_logits + b_gate).reshape(B, S, N_BRANCH, D)
    return jnp.sum(g * y, axis=2) @ w_out


def moe(xf, router_w, router_b, w1, b1, w2, b2):
    T, D = xf.shape
    logits = (xf @ router_w + router_b).astype(jnp.float32)
    top_vals, top_idx = lax.top_k(logits, TOP_K)
    gates = jax.nn.softmax(top_vals, axis=-1)
    N = T * TOP_K
    n_blocks = (N + N_EXPERTS * (MOE_BLOCK - 1) + MOE_BLOCK - 1) // MOE_BLOCK
    P = n_blocks * MOE_BLOCK
    slot_expert = top_idx.reshape(N)
    slot_token = jnp.arange(N, dtype=jnp.int32) // TOP_K
    order = jnp.argsort(slot_expert)
    e_sorted = slot_expert[order]
    counts = jnp.bincount(slot_expert, length=N_EXPERTS)
    padded = (counts + MOE_BLOCK - 1) // MOE_BLOCK * MOE_BLOCK
    starts = jnp.cumsum(counts) - counts
    pends = jnp.cumsum(padded)
    pstarts = pends - padded
    dest = pstarts[e_sorted] + (jnp.arange(N, dtype=jnp.int32) - starts[e_sorted])
    tok_pad = jnp.full((P,), T, dtype=jnp.int32).at[dest].set(slot_token[order])
    gate_pad = jnp.zeros((P,), jnp.float32).at[dest].set(gates.reshape(N)[order])
    block_expert = jnp.minimum(jnp.searchsorted(pends, jnp.arange(n_blocks) * MOE_BLOCK, side='right'),
                               N_EXPERTS - 1)
    x_pad = jnp.concatenate([xf, jnp.zeros((1, D), xf.dtype)], axis=0)
    xb = x_pad[tok_pad].reshape(n_blocks, MOE_BLOCK, D)

    def expert_block(args):
        xblk, e = args
        hgu = xblk @ w1[e] + b1[e]
        gate = jnp.minimum(hgu[..., 0::2], SWIGLU_LIMIT)
        up = jnp.clip(hgu[..., 1::2], -SWIGLU_LIMIT, SWIGLU_LIMIT)
        glu = gate * jax.nn.sigmoid(gate * SWIGLU_ALPHA)
        return ((up + 1.0) * glu) @ w2[e] + b2[e]

    yb = lax.map(expert_block, (xb, block_expert)).reshape(P, D)
    out = jnp.zeros((T + 1, D), xf.dtype).at[tok_pad].add(yb * gate_pad[:, None].astype(xf.dtype))
    return out[:T]


def setup_inputs(seed: int = 0) -> dict:
    key = jax.random.key(seed)
    ks = jax.random.split(key, 32)
    L, D, E, F = DEPTH, D_MODEL, N_EXPERTS, D_EXPERT

    def nrm(k, shape, scale):
        return jax.random.normal(k, shape, jnp.float32) * scale

    x = nrm(ks[0], (BATCH, SEQ, D), 1.0)
    c = nrm(ks[1], (BATCH, D), 1.0)
    positions = (jnp.arange(SEQ, dtype=jnp.int32)[None, :]
                 + jax.random.randint(ks[2], (BATCH, 1), 0, 4096, dtype=jnp.int32))
    return {
        "x": x,
        "c": c,
        "positions": positions,
        "norm1_g": 1.0 + nrm(ks[3], (L, D), 0.05),
        "norm2_g": 1.0 + nrm(ks[4], (L, D), 0.05),
        "w_ada": nrm(ks[5], (L, D, 6 * D), 0.5 * D ** -0.5),
        "b_ada": nrm(ks[6], (L, 6 * D), 0.02),
        "w_in": nrm(ks[7], (L, D, IN_COLS), D ** -0.5),
        "b_gate": nrm(ks[8], (L, N_BRANCH * D), 0.02),
        "mla_q_norm": 1.0 + nrm(ks[9], (L, MLA_Q_LORA), 0.05),
        "mla_w_uq": nrm(ks[10], (L, MLA_Q_LORA, MLA_HEADS * (MLA_ROPE + MLA_NOPE)), MLA_Q_LORA ** -0.5),
        "mla_kv_norm": 1.0 + nrm(ks[11], (L, MLA_KV_LORA), 0.05),
        "mla_w_ukv": nrm(ks[12], (L, MLA_KV_LORA, MLA_HEADS * (MLA_NOPE + MLA_V)), MLA_KV_LORA ** -0.5),
        "conv_w": nrm(ks[13], (L, CONV_WIDTH, CONV_DIM), CONV_WIDTH ** -0.5),
        "w_branch": nrm(ks[14], (L, N_BRANCH, BRANCH_WIDTH, D), BRANCH_WIDTH ** -0.5),
        "w_out": nrm(ks[15], (L, D, D), D ** -0.5),
        "router_w": nrm(ks[16], (L, D, E), D ** -0.5),
        "router_b": nrm(ks[17], (L, E), 0.01),
        "exp_w1": nrm(ks[18], (L, E, D, 2 * F), D ** -0.5),
        "exp_b1": nrm(ks[19], (L, E, 2 * F), 0.01),
        "exp_w2": nrm(ks[20], (L, E, F, D), F ** -0.5),
        "exp_b2": nrm(ks[21], (L, E, D), 0.01),
        "final_g": 1.0 + nrm(ks[22], (D,), 0.05),
    }


def reference(x, c, positions, norm1_g, norm2_g, w_ada, b_ada, w_in, b_gate, mla_q_norm, mla_w_uq,
              mla_kv_norm, mla_w_ukv, conv_w, w_branch, w_out, router_w, router_b, exp_w1, exp_b1,
              exp_w2, exp_b2, final_g):
    B, S, D = x.shape
    c_act = jax.nn.silu(c)
    for l in range(DEPTH):
        mod = (c_act @ w_ada[l] + b_ada[l])[:, None, :]
        sh1, sc1, g1, sh2, sc2, g2 = jnp.split(mod, 6, axis=-1)
        h = rmsnorm(x, norm1_g[l]) * (1.0 + sc1) + sh1
        x = x + g1 * token_mixer(h, positions, w_in[l], b_gate[l], mla_q_norm[l], mla_w_uq[l],
                                 mla_kv_norm[l], mla_w_ukv[l], conv_w[l], w_branch[l], w_out[l])
        h = rmsnorm(x, norm2_g[l]) * (1.0 + sc2) + sh2
        y = moe(h.reshape(B * S, D), router_w[l], router_b[l], exp_w1[l], exp_b1[l], exp_w2[l], exp_b2[l])
        x = x + g2 * y.reshape(B, S, D)
    return rmsnorm(x, final_g)
```

```python
import functools
import math

import jax
import jax.numpy as jnp
from jax import lax
from jax.experimental import pallas as pl
from jax.experimental.pallas import tpu as pltpu

F32 = jnp.float32
BF16 = jnp.bfloat16
I32 = jnp.int32

ROPE_THETA = 500000.0
NORM_EPS = 1e-6

MLA_HEADS = 8
MLA_NOPE = 64
MLA_ROPE = 32
MLA_V = 64
MLA_Q_LORA = 256
MLA_KV_LORA = 128
CONV_DIM = 512
CONV_WIDTH = 3
DSA_HEADS = 8
DSA_HEAD_DIM = 64
DSA_ROT = 16
IDX_HEADS = 8
IDX_DIM = 64
IDX_ROT = 16
TOPK_MAX = 256
N_BRANCH = 3
BRANCH_WIDTH = 512
N_EXPERTS = 32
TOP_K = 4
SWIGLU_ALPHA = 1.702
SWIGLU_LIMIT = 7.0

LANES = 128
HEAD_PAD = 128
INT_MIN = -(2 ** 31)
MASK_BIAS = -1e30
VMEM_LIMIT = 56 * 1024 * 1024

SEG_QLAT = 0
SEG_KVLAT = 256
SEG_KPE = 384
SEG_IKW = 512
SEG_CONV = 640
SEG_DQ = 2176
SEG_DKV = 2688
SEG_IQ = 2816
MAIN_COLS = 3328


def _cparams(sem):
    return pltpu.CompilerParams(dimension_semantics=sem, vmem_limit_bytes=VMEM_LIMIT)


def _rope(x, cos, sin, half, period, limit):
    lane = lax.broadcasted_iota(I32, x.shape, 1)
    first = (lane % period) < half
    rot = jnp.where(first, pltpu.roll(x, LANES - half, axis=1), pltpu.roll(x, half, axis=1))
    out = x * cos + rot * sin
    if limit < LANES:
        out = jnp.where(lane < limit, out, x)
    return out


def _ada_kernel(c_ref, w_ref, b_ref, o_ref):
    c = c_ref[...]
    act = c * jax.nn.sigmoid(c)
    o_ref[0] = jnp.dot(act, w_ref[0], preferred_element_type=F32,
                       precision=lax.Precision.HIGHEST) + b_ref[0]


def _ada(c, w_ada, b_ada):
    L, D, D6 = w_ada.shape
    B = c.shape[0]
    nblk = D6 // D
    return pl.pallas_call(
        _ada_kernel,
        grid=(L, nblk),
        in_specs=[pl.BlockSpec((B, D), lambda l, j: (0, 0)),
                  pl.BlockSpec((1, D, D), lambda l, j: (l, 0, j)),
                  pl.BlockSpec((1, 1, D), lambda l, j: (l, 0, j))],
        out_specs=pl.BlockSpec((1, B, D), lambda l, j: (l, 0, j)),
        out_shape=jax.ShapeDtypeStruct((L, B, D6), F32),
        compiler_params=_cparams(("arbitrary", "arbitrary")),
        name="adaln",
    )(c, w_ada, b_ada.reshape(L, 1, D6))


def _trig_kernel(freq_ref, pos_ref, cos_ref, sin_ref):
    ang = pos_ref[...].astype(F32) * freq_ref[pl.program_id(0)]
    cos_ref[0] = jnp.cos(ang)
    sin_ref[0] = jnp.sin(ang)


def _rope_tables(positions):
    B, S = positions.shape
    ha, hc = MLA_ROPE // 2, DSA_ROT // 2
    fa = jnp.exp(-math.log(ROPE_THETA) * jnp.arange(ha, dtype=F32) * (2.0 / MLA_ROPE))
    fc = jnp.exp(-math.log(ROPE_THETA) * jnp.arange(hc, dtype=F32) * (2.0 / DSA_ROT))
    freqs = jnp.concatenate([fa, fc])
    nf = ha + hc
    cos, sin = pl.pallas_call(
        _trig_kernel,
        grid=(nf,),
        in_specs=[pl.BlockSpec(memory_space=pltpu.SMEM),
                  pl.BlockSpec((B, S), lambda i: (0, 0))],
        out_specs=[pl.BlockSpec((1, B, S), lambda i: (i, 0, 0)),
                   pl.BlockSpec((1, B, S), lambda i: (i, 0, 0))],
        out_shape=[jax.ShapeDtypeStruct((nf, B, S), F32)] * 2,
        compiler_params=_cparams(("arbitrary",)),
        name="rope_trig",
    )(freqs, positions)
    cos = jnp.moveaxis(cos, 0, -1).reshape(B * S, nf)
    sin = jnp.moveaxis(sin, 0, -1).reshape(B * S, nf)
    T = B * S
    one = lambda n: jnp.ones((T, n), F32)
    zero = lambda n: jnp.zeros((T, n), F32)
    ca, sa, cc, sc = cos[:, :ha], sin[:, :ha], cos[:, ha:], sin[:, ha:]
    cos_a = jnp.concatenate([ca, ca, one(LANES - 2 * ha)], axis=1)
    sin_a = jnp.concatenate([-sa, sa, zero(LANES - 2 * ha)], axis=1)
    pc = jnp.concatenate([cc, cc, one(DSA_HEAD_DIM - 2 * hc)], axis=1)
    ps = jnp.concatenate([-sc, sc, zero(DSA_HEAD_DIM - 2 * hc)], axis=1)
    cos_c = jnp.concatenate([pc, pc], axis=1)
    sin_c = jnp.concatenate([ps, ps], axis=1)
    return cos_a, sin_a, cos_c, sin_c


def _front_kernel(x_ref, sh_ref, sc_ref, g_ref, w_ref, qn_ref, wuq_ref, kvn_ref, wk_ref, wv_ref, cw_ref,
                  ca_ref, sa_ref, cc_ref, sn_ref,
                  q_out, k_out, v_out, ob_out, qc_out, kvc_out, qi_out, ki_out, iw_out,
                  carry_ref, *, tiles_per_seq):
    i = pl.program_id(0)
    x = x_ref[...]
    ms = jnp.mean(x * x, axis=-1, keepdims=True)
    h = (x * lax.rsqrt(ms + NORM_EPS) * g_ref[...]) * (1.0 + sc_ref[0]) + sh_ref[0]
    hb = h.astype(BF16)
    ca, sa, cc, sn = ca_ref[...], sa_ref[...], cc_ref[...], sn_ref[...]

    def proj(start, width):
        return jnp.dot(hb, w_ref[:, start:start + width], preferred_element_type=F32)

    q_lat = proj(SEG_QLAT, MLA_Q_LORA)
    cq = q_lat * lax.rsqrt(jnp.mean(q_lat * q_lat, axis=-1, keepdims=True) + NORM_EPS) * qn_ref[...]
    cqb = cq.astype(BF16)
    kv_lat = proj(SEG_KVLAT, MLA_KV_LORA)
    ckv = kv_lat * lax.rsqrt(jnp.mean(kv_lat * kv_lat, axis=-1, keepdims=True) + NORM_EPS) * kvn_ref[...]
    ckvb = ckv.astype(BF16)
    k_pe = _rope(proj(SEG_KPE, LANES), ca, sa, MLA_ROPE // 2, LANES, LANES)
    q_scale = (MLA_NOPE + MLA_ROPE) ** -0.5
    for hd in range(MLA_HEADS):
        qh = jnp.dot(cqb, wuq_ref[hd], preferred_element_type=F32)
        qh = _rope(qh, ca, sa, MLA_ROPE // 2, LANES, LANES) * q_scale
        q_out[0, hd] = qh.astype(BF16)
        kh = jnp.dot(ckvb, wk_ref[hd], preferred_element_type=F32) + k_pe
        k_out[0, hd] = kh.astype(BF16)
        v_out[0, hd] = jnp.dot(ckvb, wv_ref[hd], preferred_element_type=F32).astype(BF16)

    g_b = proj(SEG_CONV, CONV_DIM)
    u = proj(SEG_CONV + CONV_DIM, CONV_DIM) * proj(SEG_CONV + 2 * CONV_DIM, CONV_DIM)
    tm = u.shape[0]
    @pl.when(i % tiles_per_seq == 0)
    def _():
        carry_ref[...] = jnp.zeros(carry_ref.shape, F32)

    prev = carry_ref[...]
    row =lax.broadcasted_iota(I32, u.shape, 0)
    u1 = jnp.where(row == 0, prev[7:8], pltpu.roll(u, 1, axis=0))
    u2 = jnp.where(row == 0, prev[6:7], jnp.where(row == 1, prev[7:8], pltpu.roll(u, 2, axis=0)))
    carry_ref[...] = u[tm - 8:tm]
    cw = cw_ref[...]
    y = cw[0:1] * u2 + cw[1:2] * u1 + cw[2:3] * u
    ob_out[...] = (g_b * y).astype(BF16)

    hc = DSA_ROT // 2
    dsa_scale = DSA_HEAD_DIM ** -0.5
    idx_scale = IDX_DIM ** -0.5
    for seg, scale, out in ((SEG_DQ, dsa_scale, qc_out), (SEG_IQ, idx_scale, qi_out)):
        for grp in range(DSA_HEADS * DSA_HEAD_DIM // LANES):
            piece = _rope(proj(seg + grp * LANES, LANES), cc, sn, hc, DSA_HEAD_DIM, LANES) * scale
            pb = piece.astype(BF16)
            out[0, 2 * grp] = pb[:, :DSA_HEAD_DIM]
            out[0, 2 * grp + 1] = pb[:, DSA_HEAD_DIM:]
    kvc_out[...] = _rope(proj(SEG_DKV, LANES), cc, sn, hc, DSA_HEAD_DIM, DSA_HEAD_DIM).astype(BF16)
    ikw = _rope(proj(SEG_IKW, LANES), cc, sn, hc, DSA_HEAD_DIM, IDX_DIM)
    ki_out[...] = ikw.astype(BF16)
    iw_out[...] = ikw * (IDX_HEADS ** -0.5)


def _front(x2, shift, scale, norm_g, wts, tabs, B, S, tm):
    T, D = x2.shape
    tps = S // tm
    nt = T // tm
    cos_a, sin_a, cos_c, sin_c = tabs
    row = lambda i: (i, 0)
    bat = lambda i: (i // tps, 0, 0)
    head = lambda i: (i // tps, 0, i % tps, 0)
    full2 = lambda i: (0, 0)
    full3 = lambda i: (0, 0, 0)
    H = MLA_HEADS
    in_specs = [
        pl.BlockSpec((tm, D), row),
        pl.BlockSpec((1, 1, D), bat),
        pl.BlockSpec((1, 1, D), bat),
        pl.BlockSpec((1, D), full2),
        pl.BlockSpec((D, MAIN_COLS), full2),
        pl.BlockSpec((1, MLA_Q_LORA), full2),
        pl.BlockSpec((H, MLA_Q_LORA, HEAD_PAD), full3),
        pl.BlockSpec((1, MLA_KV_LORA), full2),
        pl.BlockSpec((H, MLA_KV_LORA, HEAD_PAD), full3),
        pl.BlockSpec((H, MLA_KV_LORA, MLA_V), full3),
        pl.BlockSpec((CONV_WIDTH, CONV_DIM), full2),
        pl.BlockSpec((tm, LANES), row),
        pl.BlockSpec((tm, LANES), row),
        pl.BlockSpec((tm, LANES), row),
        pl.BlockSpec((tm, LANES), row),
    ]
    out_specs = [
        pl.BlockSpec((1, H, tm, HEAD_PAD), head),
        pl.BlockSpec((1, H, tm, HEAD_PAD), head),
        pl.BlockSpec((1, H, tm, MLA_V), head),
        pl.BlockSpec((tm, CONV_DIM), row),
        pl.BlockSpec((1, DSA_HEADS, tm, DSA_HEAD_DIM), head),
        pl.BlockSpec((tm, LANES), row),
        pl.BlockSpec((1, IDX_HEADS, tm, IDX_DIM), head),
        pl.BlockSpec((tm, LANES), row),
        pl.BlockSpec((tm, LANES), row),
    ]
    out_shape = [
        jax.ShapeDtypeStruct((B, H, S, HEAD_PAD), BF16),
        jax.ShapeDtypeStruct((B, H, S, HEAD_PAD), BF16),
        jax.ShapeDtypeStruct((B, H, S, MLA_V), BF16),
        jax.ShapeDtypeStruct((T, CONV_DIM), BF16),
        jax.ShapeDtypeStruct((B, DSA_HEADS, S, DSA_HEAD_DIM), BF16),
        jax.ShapeDtypeStruct((T, LANES), BF16),
        jax.ShapeDtypeStruct((B, IDX_HEADS, S, IDX_DIM), BF16),
        jax.ShapeDtypeStruct((T, LANES), BF16),
        jax.ShapeDtypeStruct((T, LANES), F32),
    ]
    return pl.pallas_call(
        functools.partial(_front_kernel, tiles_per_seq=tps),
        grid=(nt,),
        in_specs=in_specs,
        out_specs=out_specs,
        out_shape=out_shape,
        scratch_shapes=[pltpu.VMEM((8, CONV_DIM), F32)],
        compiler_params=_cparams(("arbitrary",)),
        name="front",
    )(x2, shift, scale, norm_g, wts["w_main"], wts["q_norm"], wts["w_uq"], wts["kv_norm"],
      wts["w_k"], wts["w_v"], wts["conv_w"], cos_a, sin_a, cos_c, sin_c)


def _mla_kernel(q_ref, k_ref, v_ref, o_ref, *, tq):
    i = pl.program_id(1)
    outs = []
    for hd in range(MLA_HEADS):
        q = q_ref[0, hd]

        def step(c, carry, masked, q=q, hd=hd):
            m, l, acc = carry
            off = pl.multiple_of(c * tq, tq)
            k = k_ref[0, hd, pl.ds(off, tq), :]
            v = v_ref[0, hd, pl.ds(off, tq), :]
            s = lax.dot_general(q, k, (((1,), (1,)), ((), ())), preferred_element_type=F32)
            if masked:
                qpos = lax.broadcasted_iota(I32, s.shape, 0)
                kpos = lax.broadcasted_iota(I32, s.shape, 1)
                s = jnp.where(kpos <= qpos, s, MASK_BIAS)
            m_new = jnp.maximum(m, jnp.max(s, axis=-1, keepdims=True))
            alpha = jnp.exp(m - m_new)
            p = jnp.exp(s - m_new)
            l = alpha * l + jnp.sum(p, axis=-1, keepdims=True)
            acc = alpha * acc + jnp.dot(p.astype(BF16), v, preferred_element_type=F32)
            return m_new, l, acc

        init = (jnp.full((tq, 1), MASK_BIAS, F32), jnp.zeros((tq, 1), F32), jnp.zeros((tq, MLA_V), F32))
        carry = lax.fori_loop(0, i, functools.partial(step, masked=False), init)
        m, l, acc = step(i, carry, True)
        outs.append(acc / l)
    o_ref[...] = jnp.concatenate(outs, axis=-1).astype(BF16)


def _mla(q, k, v, tq):
    B, H, S, _ = q.shape
    nq = S // tq
    return pl.pallas_call(
        functools.partial(_mla_kernel, tq=tq),
        grid=(B, nq),
        in_specs=[pl.BlockSpec((1, H, tq, HEAD_PAD), lambda b, i: (b, 0, i, 0)),
                  pl.BlockSpec((1, H, S, HEAD_PAD), lambda b, i: (b, 0, 0, 0)),
                  pl.BlockSpec((1, H, S, MLA_V), lambda b, i: (b, 0, 0, 0))],
        out_specs=pl.BlockSpec((tq, H * MLA_V), lambda b, i: (b * nq + i, 0)),
        out_shape=jax.ShapeDtypeStruct((B * S, H * MLA_V), BF16),
        compiler_params=_cparams(("arbitrary", "arbitrary")),
        name="mla_attention",
    )(q, k, v)


def _float_key(score):
    bits = pltpu.bitcast(score + 0.0, I32)
    return jnp.where(bits >= 0, bits, bits ^ jnp.int32(0x7FFFFFFF))


def _dsa_kernel(qi_ref, ki_ref, iw_ref, qc_ref, kv_ref, o_ref,
                key_ref, cut_ref, m_ref, l_ref, acc_ref, *, tq, topk, seq):
    i = pl.program_id(1)
    nch = i + 1
    q0 = i * tq
    qpos = q0 + lax.broadcasted_iota(I32, (tq, tq), 0)
    lane = lax.broadcasted_iota(I32, (tq, tq), 1)
    nt = (((1,), (1,)), ((), ()))

    iw = iw_ref[...]
    wcols = [iw[:, IDX_DIM + hd:IDX_DIM + hd + 1] for hd in range(IDX_HEADS)]

    def score_chunk(c, _):
        off = pl.multiple_of(c * tq, tq)
        kic = ki_ref[pl.ds(off, tq), :][:, :IDX_DIM]
        score = jnp.zeros((tq, tq), F32)
        for hd in range(IDX_HEADS):
            lg = lax.dot_general(qi_ref[0, hd], kic, nt, preferred_element_type=F32)
            score = score + jnp.maximum(lg, 0.0) * wcols[hd]
        key = jnp.where(off + lane <= qpos, _float_key(score), INT_MIN)
        key_ref[:, pl.ds(off, tq)] = key
        return 0

    lax.fori_loop(0, nch, score_chunk, 0)

    def count(pred):
        def body(c, part):
            off = pl.multiple_of(c * tq, tq)
            hit = jnp.where(pred(key_ref[:, pl.ds(off, tq)], off), 1.0, 0.0)
            for j in range(tq // LANES):
                part = part + hit[:, j * LANES:(j + 1) * LANES]
            return part
        part = lax.fori_loop(0, nch, body, jnp.zeros((tq, LANES), F32))
        return jnp.sum(part, axis=-1, keepdims=True)

    def search(it, t):
        cand = t + (jnp.int32(1) << (31 - it))
        cnt = count(lambda kc, off: kc >= cand)
        return jnp.where(cnt >= topk, cand, t)

    thr = lax.fori_loop(0, 32, search, jnp.full((tq, 1), INT_MIN, I32))

    n_gt = count(lambda kc, off: kc > thr)
    n_eq = count(lambda kc, off: kc == thr)
    need = topk - n_gt
    tie = (thr > INT_MIN) & (n_eq > need)
    cut_ref[...] = jnp.full((tq, LANES), seq, I32)

    @pl.when(jnp.max(jnp.where(tie, 1.0, 0.0)) > 0.0)
    def _():
        nbits = max(1, (seq - 1).bit_length())

        def find(it, p):
            cand = p + (jnp.int32(1) << (nbits - 1 - it))
            cnt = count(lambda kc, off: (kc == thr) & (off + lane < cand))
            return jnp.where(cnt < need, cand, p)

        cut = lax.fori_loop(0, nbits, find, jnp.zeros((tq, 1), I32))
        cut_ref[...] = jnp.broadcast_to(jnp.where(tie, cut, seq), (tq, LANES))

    m_ref[...] = jnp.full(m_ref.shape, MASK_BIAS, F32)
    l_ref[...] = jnp.zeros(l_ref.shape, F32)
    acc_ref[...] = jnp.zeros(acc_ref.shape, F32)
    thr1 = thr
    cut1 = cut_ref[:, 0:1]

    def attend(c, _):
        off = pl.multiple_of(c * tq, tq)
        key = key_ref[:, pl.ds(off, tq)]
        kpos = off + lane
        sel = ((key > thr1) | ((key == thr1) & (kpos <= cut1))) & (kpos <= qpos)
        bias = jnp.where(sel, 0.0, MASK_BIAS)
        kvc = kv_ref[pl.ds(off, tq), :]
        kc = kvc[:, :DSA_HEAD_DIM]
        vc = kvc[:, DSA_HEAD_DIM:]
        for hd in range(DSA_HEADS):
            s = lax.dot_general(qc_ref[0, hd], kc, nt, preferred_element_type=F32) + bias
            m = m_ref[hd]
            m_new = jnp.maximum(m, jnp.max(s, axis=-1, keepdims=True))
            alpha = jnp.exp(m - m_new)
            p = jnp.where(sel, jnp.exp(s - m_new), 0.0)
            l_ref[hd] = alpha * l_ref[hd] + jnp.sum(p, axis=-1, keepdims=True)
            acc_ref[hd] = alpha * acc_ref[hd] + jnp.dot(p.astype(BF16), vc, preferred_element_type=F32)
            m_ref[hd] = m_new
        return 0

    lax.fori_loop(0, nch, attend, 0)
    o_ref[...] = jnp.concatenate([acc_ref[hd] / l_ref[hd] for hd in range(DSA_HEADS)], axis=-1).astype(BF16)


def _dsa(qi, ki, iw, qc, kvc, tq):
    B, H, S, _ = qc.shape
    nq = S // tq
    topk = min(TOPK_MAX, S // 4)
    return pl.pallas_call(
        functools.partial(_dsa_kernel, tq=tq, topk=topk, seq=S),
        grid=(B, nq),
        in_specs=[pl.BlockSpec((1, IDX_HEADS, tq, IDX_DIM), lambda b, i: (b, 0, i, 0)),
                  pl.BlockSpec((S, LANES), lambda b, i: (b, 0)),
                  pl.BlockSpec((tq, LANES), lambda b, i: (b * nq + i, 0)),
                  pl.BlockSpec((1, H, tq, DSA_HEAD_DIM), lambda b, i: (b, 0, i, 0)),
                  pl.BlockSpec((S, LANES), lambda b, i: (b, 0))],
        out_specs=pl.BlockSpec((tq, H * DSA_HEAD_DIM), lambda b, i: (b * nq + i, 0)),
        out_shape=jax.ShapeDtypeStruct((B * S, H * DSA_HEAD_DIM), BF16),
        scratch_shapes=[pltpu.VMEM((tq, S), I32),
                        pltpu.VMEM((tq, LANES), I32),
                        pltpu.VMEM((H, tq, 1), F32),
                        pltpu.VMEM((H, tq, 1), F32),
                        pltpu.VMEM((H, tq, DSA_HEAD_DIM), F32)],
        compiler_params=_cparams(("arbitrary", "arbitrary")),
        name="dsa_attention",
    )(qi, ki, iw, qc, kvc)


def _merge_kernel(x_ref, sh1_ref, sc1_ref, g1_ref, sh2_ref, sc2_ref, n1_ref, n2_ref,
                  oa_ref, ob_ref, oc_ref, wg_ref, bg_ref, wb_ref, wo_ref, rw_ref, rb_ref,
                  xo_ref, h2_ref, idx_ref, gate_ref):
    x = x_ref[...]
    D = x.shape[1]
    ms = jnp.mean(x * x, axis=-1, keepdims=True)
    h = (x * lax.rsqrt(ms + NORM_EPS) * n1_ref[...]) * (1.0 + sc1_ref[0]) + sh1_ref[0]
    hb = h.astype(BF16)
    mix = jnp.zeros(x.shape, F32)
    for n, o_ref in enumerate((oa_ref, ob_ref, oc_ref)):
        y = jnp.dot(o_ref[...], wb_ref[n], preferred_element_type=F32)
        gl = jnp.dot(hb, wg_ref[:, n * D:(n + 1) * D], preferred_element_type=F32) + bg_ref[:, n * D:(n + 1) * D]
        mix = mix + jax.nn.sigmoid(gl) * y
    out = jnp.dot(mix.astype(BF16), wo_ref[...], preferred_element_type=F32)
    xn = x + g1_ref[0] * out
    xo_ref[...] = xn
    ms2 = jnp.mean(xn * xn, axis=-1, keepdims=True)
    h2 = (xn * lax.rsqrt(ms2 + NORM_EPS) * n2_ref[...]) * (1.0 + sc2_ref[0]) + sh2_ref[0]
    h2_ref[...] = h2
    logits = lax.dot_general(rw_ref[...], h2.astype(BF16), (((1,), (1,)), ((), ())),
                             preferred_element_type=F32) + rb_ref[...]
    eidx = lax.broadcasted_iota(I32, logits.shape, 0)
    vals, idxs = [], []
    for _ in range(TOP_K):
        mx = jnp.max(logits, axis=0, keepdims=True)
        am = jnp.min(jnp.where(logits == mx, eidx, N_EXPERTS), axis=0, keepdims=True)
        vals.append(mx)
        idxs.append(am)
        logits = jnp.where(eidx == am, -jnp.inf, logits)
    ex = [jnp.exp(v - vals[0]) for v in vals]
    den = ex[0] + ex[1] + ex[2] + ex[3]
    pad_i = [jnp.zeros_like(idxs[0])] * (8 - TOP_K)
    pad_f = [jnp.zeros_like(den)] * (8 - TOP_K)
    idx_ref[...] = jnp.concatenate(idxs + pad_i, axis=0)
    gate_ref[...] = jnp.concatenate([e / den for e in ex] + pad_f, axis=0)


def _merge(x2, mods, n1, n2, oa, ob, oc, wts, B, S, tm):
    T, D = x2.shape
    tps = S // tm
    row = lambda i: (i, 0)
    bat = lambda i: (i // tps, 0, 0)
    full2 = lambda i: (0, 0)
    full3 = lambda i: (0, 0, 0)
    mod_spec = pl.BlockSpec((1, 1, D), bat)
    W = BRANCH_WIDTH
    return pl.pallas_call(
        _merge_kernel,
        grid=(T // tm,),
        in_specs=[pl.BlockSpec((tm, D), row)] + [mod_spec] * 5 + [
            pl.BlockSpec((1, D), full2), pl.BlockSpec((1, D), full2),
            pl.BlockSpec((tm, W), row), pl.BlockSpec((tm, W), row), pl.BlockSpec((tm, W), row),
            pl.BlockSpec((D, N_BRANCH * D), full2), pl.BlockSpec((1, N_BRANCH * D), full2),
            pl.BlockSpec((N_BRANCH, W, D), full3), pl.BlockSpec((D, D), full2),
            pl.BlockSpec((N_EXPERTS, D), full2), pl.BlockSpec((N_EXPERTS, 1), full2)],
        out_specs=[pl.BlockSpec((tm, D), row), pl.BlockSpec((tm, D), row),
                   pl.BlockSpec((8, tm), lambda i: (0, i)), pl.BlockSpec((8, tm), lambda i: (0, i))],
        out_shape=[jax.ShapeDtypeStruct((T, D), F32), jax.ShapeDtypeStruct((T, D), F32),
                   jax.ShapeDtypeStruct((8, T), I32), jax.ShapeDtypeStruct((8, T), F32)],
        compiler_params=_cparams(("arbitrary",)),
        name="merge_router",
    )(x2, *mods, n1, n2, oa, ob, oc, wts["w_gate"], wts["b_gate"], wts["w_branch"], wts["w_out"],
      wts["router_wt"], wts["router_b"])


def _moe_kernel(be_ref, tokc_ref, tokn_ref, dst_ref, x_hbm, w1g_ref, w1u_ref, w2_ref, b1g_ref, b1u_ref, b2_ref,
                y_hbm, xbuf, ybuf, gsem, ssem, *, tb, nblk, hchunk):
    i = pl.program_id(0)
    slot = i % 2

    def gather(tok_ref, s):
        def body(r, _):
            pltpu.make_async_copy(x_hbm.at[pl.ds(tok_ref[0, 0, r], 1)], xbuf.at[s, pl.ds(r, 1)],
                                  gsem.at[s]).start()
            return 0
        lax.fori_loop(0, tb, body, 0)

    @pl.when(i == 0)
    def _():
        gather(tokc_ref, 0)

    @pl.when(i + 1 < nblk)
    def _():
        gather(tokn_ref, 1 - slot)

    pltpu.make_async_copy(x_hbm.at[pl.ds(0, tb)], xbuf.at[slot], gsem.at[slot]).wait()

    @pl.when(i >= 2)
    def _():
        pltpu.make_async_copy(ybuf.at[slot], y_hbm.at[pl.ds(0, tb)], ssem.at[slot]).wait()

    xb = xbuf[slot].astype(BF16)
    D = xb.shape[1]
    y = jnp.zeros((tb, D), F32) + b2_ref[0]
    for j in range(D // hchunk):
        cs = slice(j * hchunk, (j + 1) * hchunk)
        hg = jnp.dot(xb, w1g_ref[0, :, cs], preferred_element_type=F32) + b1g_ref[0, :, cs]
        hu = jnp.dot(xb, w1u_ref[0, :, cs], preferred_element_type=F32) + b1u_ref[0, :, cs]
        gate = jnp.minimum(hg, SWIGLU_LIMIT)
        up = jnp.clip(hu, -SWIGLU_LIMIT, SWIGLU_LIMIT)
        act = (up + 1.0) * (gate * jax.nn.sigmoid(gate * SWIGLU_ALPHA))
        y = y + jnp.dot(act.astype(BF16), w2_ref[0, cs, :], preferred_element_type=F32)
    ybuf[slot] = y

    def scatter(r, _):
        pltpu.make_async_copy(ybuf.at[slot, pl.ds(r, 1)], y_hbm.at[pl.ds(dst_ref[0, 0, r], 1)],
                              ssem.at[slot]).start()
        return 0
    lax.fori_loop(0, tb, scatter, 0)

    @pl.when(i == nblk - 1)
    def _():
        pltpu.make_async_copy(ybuf.at[slot], y_hbm.at[pl.ds(0, tb)], ssem.at[slot]).wait()
        if nblk > 1:
            pltpu.make_async_copy(ybuf.at[1 - slot], y_hbm.at[pl.ds(0, tb)], ssem.at[1 - slot]).wait()


def _moe(h2, block_expert, tok_pad, dst_pad, wts, tb):
    _, D = h2.shape
    P = tok_pad.shape[0]
    nblk = P // tb
    tok3 = tok_pad.reshape(nblk, 1, tb)
    dst3 = dst_pad.reshape(nblk, 1, tb)
    smem_blk = lambda f: pl.BlockSpec((1, 1, tb), f, memory_space=pltpu.SMEM)
    wspec = pl.BlockSpec((1, D, D), lambda i, be: (be[i], 0, 0))
    bspec = pl.BlockSpec((1, 1, D), lambda i, be: (be[i], 0, 0))
    grid_spec = pltpu.PrefetchScalarGridSpec(
        num_scalar_prefetch=1,
        grid=(nblk,),
        in_specs=[smem_blk(lambda i, be: (i, 0, 0)),
                  smem_blk(lambda i, be: (jnp.minimum(i + 1, nblk - 1), 0, 0)),
                  smem_blk(lambda i, be: (i, 0, 0)),
                  pl.BlockSpec(memory_space=pl.ANY),
                  wspec, wspec, wspec, bspec, bspec, bspec],
        out_specs=pl.BlockSpec(memory_space=pl.ANY),
        scratch_shapes=[pltpu.VMEM((2, tb, D), F32), pltpu.VMEM((2, tb, D), F32),
                        pltpu.SemaphoreType.DMA((2,)), pltpu.SemaphoreType.DMA((2,))],
    )
    return pl.pallas_call(
        functools.partial(_moe_kernel, tb=tb, nblk=nblk, hchunk=256),
        grid_spec=grid_spec,
        out_shape=jax.ShapeDtypeStruct((P, D), F32),
        compiler_params=_cparams(("arbitrary",)),
        name="moe_ffn",
    )(block_expert, tok3, tok3, dst3, h2, wts["w1g"], wts["w1u"], wts["w2"], wts["b1g"], wts["b1u"], wts["b2"])


def _route(top_idx, T, tb):
    E = N_EXPERTS
    N = T * TOP_K
    nblk = (N + E * (tb - 1) + tb - 1) // tb
    P = nblk * tb
    flat_e = top_idx.reshape(N)
    order = jnp.argsort(flat_e).astype(I32)
    e_sorted = flat_e[order]
    counts = jnp.bincount(flat_e, length=E).astype(I32)
    padded = (counts + tb - 1) // tb * tb
    starts = jnp.cumsum(counts) - counts
    pends = jnp.cumsum(padded)
    pstarts = pends - padded
    dest = pstarts[e_sorted] + (jnp.arange(N, dtype=I32) - starts[e_sorted])
    slot_at = jnp.full((P,), -1, I32).at[dest].set(order)
    is_pad = slot_at < 0
    pad_rank = jnp.cumsum(is_pad.astype(I32)) - 1
    tok_pad = jnp.where(is_pad, 0, slot_at % T)
    dst_pad = jnp.where(is_pad, N + pad_rank, slot_at)
    block_expert = jnp.minimum(jnp.searchsorted(pends, jnp.arange(nblk, dtype=I32) * tb, side='right'),
                               E - 1).astype(I32)
    return block_expert, tok_pad, dst_pad


def _combine_kernel(x_ref, g2_ref, gate_ref, y0_ref, y1_ref, y2_ref, y3_ref, fg_ref, o_ref, *, final):
    g = gate_ref[...]
    y = (g[:, 0:1] * y0_ref[...] + g[:, 1:2] * y1_ref[...]
         + g[:, 2:3] * y2_ref[...] + g[:, 3:4] * y3_ref[...])
    x = x_ref[...] + g2_ref[0] * y
    if final:
        x = x * lax.rsqrt(jnp.mean(x * x, axis=-1, keepdims=True) + NORM_EPS) * fg_ref[...]
    o_ref[...] = x


def _combine(x2, g2, gates, y4, final_g, B, S, tm, final):
    T, D = x2.shape
    tps = S // tm
    nt = T // tm
    row = lambda i: (i, 0)
    yspec = lambda k: pl.BlockSpec((tm, D), lambda i, k=k: (k * nt + i, 0))
    return pl.pallas_call(
        functools.partial(_combine_kernel, final=final),
        grid=(nt,),
        in_specs=[pl.BlockSpec((tm, D), row),
                  pl.BlockSpec((1, 1, D), lambda i: (i // tps, 0, 0)),
                  pl.BlockSpec((tm, TOP_K), row),
                  yspec(0), yspec(1), yspec(2), yspec(3),
                  pl.BlockSpec((1, D), lambda i: (0, 0))],
        out_specs=pl.BlockSpec((tm, D), row),
        out_shape=jax.ShapeDtypeStruct((T, D), F32),
        compiler_params=_cparams(("arbitrary",)),
        name="combine",
    )(x2, g2, gates, y4, y4, y4, y4, final_g)


def _pack_layer(l, w_in, b_gate, mla_q_norm, mla_w_uq, mla_kv_norm, mla_w_ukv, conv_w, w_branch, w_out,
                router_w, router_b, exp_w1, exp_b1, exp_w2, exp_b2):
    D = w_in.shape[1]
    wi = w_in[l]
    o = [0]
    def seg(n):
        s = wi[:, o[0]:o[0] + n]
        o[0] += n
        return s
    q_lat, kv_full, conv = seg(MLA_Q_LORA), seg(MLA_KV_LORA + MLA_ROPE), seg(3 * CONV_DIM)
    dq, dkv, iq = seg(DSA_HEADS * DSA_HEAD_DIM), seg(2 * DSA_HEAD_DIM), seg(IDX_HEADS * IDX_DIM)
    ik, iw, gates = seg(IDX_DIM), seg(IDX_HEADS), seg(N_BRANCH * D)
    zeros = lambda n: jnp.zeros((D, n), wi.dtype)
    w_main = jnp.concatenate([
        q_lat, kv_full[:, :MLA_KV_LORA],
        kv_full[:, MLA_KV_LORA:], zeros(LANES - MLA_ROPE),
        ik, iw, zeros(LANES - IDX_DIM - IDX_HEADS),
        conv, dq, dkv, iq], axis=1).astype(BF16)
    assert w_main.shape[1] == MAIN_COLS
    H = MLA_HEADS
    hq = MLA_ROPE + MLA_NOPE
    w_uq = mla_w_uq[l].reshape(MLA_Q_LORA, H, hq)
    w_uq = jnp.pad(w_uq, ((0, 0), (0, 0), (0, HEAD_PAD - hq))).transpose(1, 0, 2).astype(BF16)
    w_ukv = mla_w_ukv[l].reshape(MLA_KV_LORA, H, MLA_NOPE + MLA_V)
    w_k = jnp.pad(w_ukv[:, :, :MLA_NOPE], ((0, 0), (0, 0), (MLA_ROPE, HEAD_PAD - hq)))
    w_k = w_k.transpose(1, 0, 2).astype(BF16)
    w_v = w_ukv[:, :, MLA_NOPE:].transpose(1, 0, 2).astype(BF16)
    w1 = exp_w1[l]
    b1 = exp_b1[l]
    E = w1.shape[0]
    return dict(
        w_main=w_main,
        q_norm=mla_q_norm[l].reshape(1, -1), kv_norm=mla_kv_norm[l].reshape(1, -1),
        w_uq=w_uq, w_k=w_k, w_v=w_v, conv_w=conv_w[l],
        w_gate=gates.astype(BF16), b_gate=b_gate[l].reshape(1, -1),
        w_branch=w_branch[l].astype(BF16), w_out=w_out[l].astype(BF16),
        router_wt=router_w[l].T.astype(BF16), router_b=router_b[l].reshape(-1, 1),
        w1g=w1[:, :, 0::2].astype(BF16), w1u=w1[:, :, 1::2].astype(BF16), w2=exp_w2[l].astype(BF16),
        b1g=b1[:, 0::2].reshape(E, 1, -1), b1u=b1[:, 1::2].reshape(E, 1, -1),
        b2=exp_b2[l].reshape(E, 1, -1),
    )


def _tiles(S, T):
    tm = min(512, S)
    tq = min(256, S)
    N = T * TOP_K
    tb = 512 if N >= 512 * N_EXPERTS else 128
    return tm, tq, tb


def kernel(x, c, positions, norm1_g, norm2_g, w_ada, b_ada, w_in, b_gate, mla_q_norm, mla_w_uq, mla_kv_norm,
           mla_w_ukv, conv_w, w_branch, w_out, router_w, router_b, exp_w1, exp_b1, exp_w2, exp_b2, final_g):
    B, S, D = x.shape
    T = B * S
    L = w_ada.shape[0]
    tm, tq, tb = _tiles(S, T)
    mod = _ada(c, w_ada, b_ada)
    tabs = _rope_tables(positions)
    x2 = x.reshape(T, D)
    for l in range(L):
        wts = _pack_layer(l, w_in, b_gate, mla_q_norm, mla_w_uq, mla_kv_norm, mla_w_ukv, conv_w, w_branch,
                          w_out, router_w, router_b, exp_w1, exp_b1, exp_w2, exp_b2)
        sh1, sc1, g1, sh2, sc2, g2 = [m.reshape(B, 1, D) for m in jnp.split(mod[l], 6, axis=-1)]
        n1 = norm1_g[l].reshape(1, D)
        n2 = norm2_g[l].reshape(1, D)
        q, k, v, ob, qc, kvc, qi, ki, iw = _front(x2, sh1, sc1, n1, wts, tabs, B, S, tm)
        oa = _mla(q, k, v, tq)
        oc = _dsa(qi, ki, iw, qc, kvc, tq)
        x2, h2, idx8, gate8 = _merge(x2, (sh1, sc1, g1, sh2, sc2), n1, n2, oa, ob, oc, wts, B, S, tm)
        block_expert, tok_pad, dst_pad = _route(idx8[:TOP_K], T, tb)
        y4 = _moe(h2, block_expert, tok_pad, dst_pad, wts, tb)
        x2 = _combine(x2, g2, gate8[:TOP_K].T, y4, final_g.reshape(1, D), B, S, tm, final=(l == L - 1))
    return x2.reshape(B, S, D)
```

```python
import functools
import math

import jax
import jax.numpy as jnp
from jax import lax
from jax.experimental import pallas as pl
from jax.experimental.pallas import tpu as pltpu

F32 = jnp.float32
BF16 = jnp.bfloat16
I32 = jnp.int32

ROPE_THETA = 500000.0
NORM_EPS = 1e-6

MLA_HEADS = 8
MLA_NOPE = 64
MLA_ROPE = 32
MLA_V = 64
MLA_Q_LORA = 256
MLA_KV_LORA = 128
CONV_DIM = 512
CONV_WIDTH = 3
DSA_HEADS = 8
DSA_HEAD_DIM = 64
DSA_ROT = 16
IDX_HEADS = 8
IDX_DIM = 64
IDX_ROT = 16
TOPK_MAX = 256
N_BRANCH = 3
BRANCH_WIDTH = 512
N_EXPERTS = 32
TOP_K = 4
SWIGLU_ALPHA = 1.702
SWIGLU_LIMIT = 7.0

LANES = 128
SUBLANES = 8
HEAD_PAD = 128
INT_MIN = -(2 ** 31)
MASK_BIAS = -1e30
LOG2E = math.log2(math.e)
VMEM_LIMIT = 56 * 1024 * 1024
NT = (((1,), (1,)), ((), ()))

SEG_QLAT = 0
SEG_KVLAT = 256
SEG_KPE = 384
SEG_KK = 512
SEG_CONV = 640
MAIN_COLS = 2176
ROW_DQ = 0
ROW_IQ = 512
ROW_VC = 1024
ROW_IW = 1088
T_ROWS = 1104
DEINT = 256


def _cparams(sem):
    return pltpu.CompilerParams(dimension_semantics=sem, vmem_limit_bytes=VMEM_LIMIT)


def _rope(x, cos, sin, half, period):
    lane = lax.broadcasted_iota(I32, x.shape, 1)
    first = (lane % period) < half
    rot = jnp.where(first, pltpu.roll(x, LANES - half, axis=1), pltpu.roll(x, half, axis=1))
    return x * cos + rot * sin


def _rope_t(x, cos, sin, half, period):
    R = x.shape[0]
    row = lax.broadcasted_iota(I32, x.shape, 0)
    first = (row % period) < half
    rot = jnp.where(first, pltpu.roll(x, R - half, axis=0), pltpu.roll(x, half, axis=0))
    reps = R // period
    if reps > 1:
        cos = jnp.tile(cos, (reps, 1))
        sin = jnp.tile(sin, (reps, 1))
    return x * cos + rot * sin


def _ada_kernel(c_ref, w_ref, b_ref, o_ref):
    c = c_ref[...]
    act = c * jax.nn.sigmoid(c)
    o_ref[0] = jnp.dot(act, w_ref[0], preferred_element_type=F32,
                       precision=lax.Precision.HIGHEST) + b_ref[0]


def _ada(c, w_ada, b_ada):
    L, D, D6 = w_ada.shape
    B = c.shape[0]
    nblk = D6 // D
    return pl.pallas_call(
        _ada_kernel,
        grid=(L, nblk),
        in_specs=[pl.BlockSpec((B, D), lambda l, j: (0, 0)),
                  pl.BlockSpec((1, D, D), lambda l, j: (l, 0, j)),
                  pl.BlockSpec((1, 1, D), lambda l, j: (l, 0, j))],
        out_specs=pl.BlockSpec((1, B, D), lambda l, j: (l, 0, j)),
        out_shape=jax.ShapeDtypeStruct((L, B, D6), F32),
        compiler_params=_cparams(("arbitrary", "arbitrary")),
        name="adaln",
    )(c, w_ada, b_ada.reshape(L, 1, D6))


def _trig_kernel(freq_ref, pos_ref, cos_ref, sin_ref):
    ang = pos_ref[...].astype(F32) * freq_ref[pl.program_id(0)]
    cos_ref[0] = jnp.cos(ang)
    sin_ref[0] = jnp.sin(ang)


def _rope_tables(positions):
    B, S = positions.shape
    T = B * S
    ha, hc = MLA_ROPE // 2, DSA_ROT // 2
    fa = jnp.exp(-math.log(ROPE_THETA) * jnp.arange(ha, dtype=F32) * (2.0 / MLA_ROPE))
    fc = jnp.exp(-math.log(ROPE_THETA) * jnp.arange(hc, dtype=F32) * (2.0 / DSA_ROT))
    freqs = jnp.concatenate([fa, fc])
    nf = ha + hc
    cos, sin = pl.pallas_call(
        _trig_kernel,
        grid=(nf,),
        in_specs=[pl.BlockSpec(memory_space=pltpu.SMEM),
                  pl.BlockSpec((B, S), lambda i: (0, 0))],
        out_specs=[pl.BlockSpec((1, B, S), lambda i: (i, 0, 0)),
                   pl.BlockSpec((1, B, S), lambda i: (i, 0, 0))],
        out_shape=[jax.ShapeDtypeStruct((nf, B, S), F32)] * 2,
        compiler_params=_cparams(("arbitrary",)),
        name="rope_trig",
    )(freqs, positions)
    cos = cos.reshape(nf, T)
    sin = sin.reshape(nf, T)
    ca, sa, cc, sc = cos[:ha], sin[:ha], cos[ha:], sin[ha:]
    cos_at = jnp.concatenate([ca, ca, jnp.ones((HEAD_PAD - 2 * ha, T), F32)], axis=0)
    sin_at = jnp.concatenate([-sa, sa, jnp.zeros((HEAD_PAD - 2 * ha, T), F32)], axis=0)
    cos_ct = jnp.concatenate([cc, cc, jnp.ones((DSA_HEAD_DIM - 2 * hc, T), F32)], axis=0)
    sin_ct = jnp.concatenate([-sc, sc, jnp.zeros((DSA_HEAD_DIM - 2 * hc, T), F32)], axis=0)
    cos_a, sin_a = cos_at.T, sin_at.T
    cos_c = jnp.concatenate([cos_ct, cos_ct], axis=0).T
    sin_c = jnp.concatenate([sin_ct, sin_ct], axis=0).T
    return dict(cos_a=cos_a, sin_a=sin_a, cos_c=cos_c, sin_c=sin_c,
                cos_at=cos_at, sin_at=sin_at, cos_ct=cos_ct, sin_ct=sin_ct)


def _front_kernel(x_ref, sh_ref, sc_ref, g_ref, w_ref, wt_ref, qn_ref, wuq_ref, kvn_ref, wk_ref, wv_ref, cw_ref,
                  ca_ref, sa_ref, cc_ref, sn_ref, cat_ref, sat_ref, cct_ref, snt_ref,
                  q_out, k_out, v_out, ob_out, qc_out, qi_out, kk_out, vc_out, iw_out,
                  carry_ref, *, tiles_per_seq):
    i = pl.program_id(0)
    x = x_ref[...]
    ms = jnp.mean(x * x, axis=-1, keepdims=True)
    h = (x * lax.rsqrt(ms + NORM_EPS) * g_ref[...]) * (1.0 + sc_ref[0]) + sh_ref[0]
    hb = h.astype(BF16)

    def proj(start, width):
        return jnp.dot(hb, w_ref[:, start:start + width], preferred_element_type=F32)

    def proj_t(start, rows):
        return lax.dot_general(wt_ref[start:start + rows, :], hb, NT, preferred_element_type=F32)

    q_lat = proj(SEG_QLAT, MLA_Q_LORA)
    cq = q_lat * lax.rsqrt(jnp.mean(q_lat * q_lat, axis=-1, keepdims=True) + NORM_EPS) * qn_ref[...]
    cqb = cq.astype(BF16)
    kv_lat = proj(SEG_KVLAT, MLA_KV_LORA)
    ckv = kv_lat * lax.rsqrt(jnp.mean(kv_lat * kv_lat, axis=-1, keepdims=True) + NORM_EPS) * kvn_ref[...]
    ckvb = ckv.astype(BF16)
    k_pe = _rope(proj(SEG_KPE, LANES), ca_ref[...], sa_ref[...], MLA_ROPE // 2, LANES)
    cat, sat = cat_ref[...], sat_ref[...]
    q_scale = (MLA_NOPE + MLA_ROPE) ** -0.5 * LOG2E
    for hd in range(MLA_HEADS):
        qh = lax.dot_general(wuq_ref[hd], cqb, NT, preferred_element_type=F32)
        q_out[0, hd] = (_rope_t(qh, cat, sat, MLA_ROPE // 2, HEAD_PAD) * q_scale).astype(BF16)
        kh = jnp.dot(ckvb, wk_ref[hd], preferred_element_type=F32) + k_pe
        k_out[0, hd] = kh.astype(BF16)
        v_out[0, hd] = lax.dot_general(wv_ref[hd], ckvb, NT, preferred_element_type=F32).astype(BF16)

    g_b = proj(SEG_CONV, CONV_DIM)
    u = proj(SEG_CONV + CONV_DIM, CONV_DIM) * proj(SEG_CONV + 2 * CONV_DIM, CONV_DIM)
    tm = u.shape[0]

    @pl.when(i % tiles_per_seq == 0)
    def _():
        carry_ref[...] = jnp.zeros(carry_ref.shape, F32)

    prev = carry_ref[...]
    row = lax.broadcasted_iota(I32, u.shape, 0)
    u1 = jnp.where(row == 0, prev[7:8], pltpu.roll(u, 1, axis=0))
    u2 = jnp.where(row == 0, prev[6:7], jnp.where(row == 1, prev[7:8], pltpu.roll(u, 2, axis=0)))
    carry_ref[...] = u[tm - SUBLANES:tm]
    cw = cw_ref[...]
    y = cw[0:1] * u2 + cw[1:2] * u1 + cw[2:3] * u
    ob_out[...] = (g_b * y).astype(BF16)

    hc = DSA_ROT // 2
    cct, snt = cct_ref[...], snt_ref[...]
    nq_rows = DSA_HEADS * DSA_HEAD_DIM
    qc_scale = DSA_HEAD_DIM ** -0.5 * LOG2E
    qc_out[0] = (_rope_t(proj_t(ROW_DQ, nq_rows), cct, snt, hc, DSA_HEAD_DIM) * qc_scale).astype(BF16)
    qi_out[0] = (_rope_t(proj_t(ROW_IQ, nq_rows), cct, snt, hc, IDX_DIM) * IDX_DIM ** -0.5).astype(BF16)
    kk_out[...] = _rope(proj(SEG_KK, LANES), cc_ref[...], sn_ref[...], hc, DSA_HEAD_DIM).astype(BF16)
    vc_out[0] = proj_t(ROW_VC, DSA_HEAD_DIM).astype(BF16)
    iw_out[0] = proj_t(ROW_IW, IDX_HEADS) * IDX_HEADS ** -0.5


def _front(x2, shift, scale, norm_g, wts, tabs, B, S, tm):
    T, D = x2.shape
    tps = S // tm
    nt = T // tm
    row = lambda i: (i, 0)
    col = lambda i: (0, i)
    bat = lambda i: (i // tps, 0, 0)
    head = lambda i: (i // tps, 0, i % tps, 0)
    head_t = lambda i: (i // tps, 0, 0, i % tps)
    seq_t = lambda i: (i // tps, 0, i % tps)
    full2 = lambda i: (0, 0)
    full3 = lambda i: (0, 0, 0)
    H = MLA_HEADS
    QR = DSA_HEADS * DSA_HEAD_DIM
    in_specs = [
        pl.BlockSpec((tm, D), row),
        pl.BlockSpec((1, 1, D), bat),
        pl.BlockSpec((1, 1, D), bat),
        pl.BlockSpec((1, D), full2),
        pl.BlockSpec((D, MAIN_COLS), full2),
        pl.BlockSpec((T_ROWS, D), full2),
        pl.BlockSpec((1, MLA_Q_LORA), full2),
        pl.BlockSpec((H, HEAD_PAD, MLA_Q_LORA), full3),
        pl.BlockSpec((1, MLA_KV_LORA), full2),
        pl.BlockSpec((H, MLA_KV_LORA, HEAD_PAD), full3),
        pl.BlockSpec((H, MLA_V, MLA_KV_LORA), full3),
        pl.BlockSpec((CONV_WIDTH, CONV_DIM), full2),
        pl.BlockSpec((tm, LANES), row),
        pl.BlockSpec((tm, LANES), row),
        pl.BlockSpec((tm, LANES), row),
        pl.BlockSpec((tm, LANES), row),
        pl.BlockSpec((HEAD_PAD, tm), col),
        pl.BlockSpec((HEAD_PAD, tm), col),
        pl.BlockSpec((DSA_HEAD_DIM, tm), col),
        pl.BlockSpec((DSA_HEAD_DIM, tm), col),
    ]
    out_specs = [
        pl.BlockSpec((1, H, HEAD_PAD, tm), head_t),
        pl.BlockSpec((1, H, tm, HEAD_PAD), head),
        pl.BlockSpec((1, H, MLA_V, tm), head_t),
        pl.BlockSpec((tm, CONV_DIM), row),
        pl.BlockSpec((1, QR, tm), seq_t),
        pl.BlockSpec((1, QR, tm), seq_t),
        pl.BlockSpec((tm, LANES), row),
        pl.BlockSpec((1, DSA_HEAD_DIM, tm), seq_t),
        pl.BlockSpec((1, IDX_HEADS, tm), seq_t),
    ]
    out_shape = [
        jax.ShapeDtypeStruct((B, H, HEAD_PAD, S), BF16),
        jax.ShapeDtypeStruct((B, H, S, HEAD_PAD), BF16),
        jax.ShapeDtypeStruct((B, H, MLA_V, S), BF16),
        jax.ShapeDtypeStruct((T, CONV_DIM), BF16),
        jax.ShapeDtypeStruct((B, QR, S), BF16),
        jax.ShapeDtypeStruct((B, QR, S), BF16),
        jax.ShapeDtypeStruct((T, LANES), BF16),
        jax.ShapeDtypeStruct((B, DSA_HEAD_DIM, S), BF16),
        jax.ShapeDtypeStruct((B, IDX_HEADS, S), F32),
    ]
    return pl.pallas_call(
        functools.partial(_front_kernel, tiles_per_seq=tps),
        grid=(nt,),
        in_specs=in_specs,
        out_specs=out_specs,
        out_shape=out_shape,
        scratch_shapes=[pltpu.VMEM((SUBLANES, CONV_DIM), F32)],
        compiler_params=_cparams(("arbitrary",)),
        name="front",
    )(x2, shift, scale, norm_g, wts["w_main"], wts["w_t"], wts["q_norm"], wts["w_uq"], wts["kv_norm"],
      wts["w_k"], wts["w_v"], wts["conv_w"],
      tabs["cos_a"], tabs["sin_a"], tabs["cos_c"], tabs["sin_c"],
      tabs["cos_at"], tabs["sin_at"], tabs["cos_ct"], tabs["sin_ct"])


def _attend_heads(key_tile, q_tile, v_tile, bias, m_ref, l_ref, acc_ref, nheads, vdim):
    scores = [jnp.dot(key_tile(hd), q_tile(hd), preferred_element_type=F32) for hd in range(nheads)]
    probs, alphas = [], []
    for hd in range(nheads):
        s = scores[hd] if bias is None else scores[hd] + bias
        m_new = jnp.maximum(m_ref[hd], jnp.max(s, axis=0, keepdims=True))
        alpha = jnp.exp2(m_ref[hd] - m_new)
        p = jnp.exp2(s - m_new[0:1])
        l_ref[hd] = alpha * l_ref[hd] + jnp.sum(p, axis=0, keepdims=True)
        m_ref[hd] = m_new
        probs.append(p.astype(BF16))
        alphas.append(alpha[0:1])
    for hd in range(nheads):
        rows = slice(hd * vdim, (hd + 1) * vdim)
        acc_ref[rows, :] = alphas[hd] * acc_ref[rows, :] + jnp.dot(v_tile(hd), probs[hd],
                                                                   preferred_element_type=F32)


def _attend_init(m_ref, l_ref, acc_ref):
    m_ref[...] = jnp.full(m_ref.shape, MASK_BIAS, F32)
    l_ref[...] = jnp.zeros(l_ref.shape, F32)
    acc_ref[...] = jnp.zeros(acc_ref.shape, F32)


def _attend_finish(o_ref, l_ref, acc_ref, nheads, vdim):
    outs = [acc_ref[hd * vdim:(hd + 1) * vdim, :] / l_ref[hd, 0:1, :] for hd in range(nheads)]
    o_ref[...] = jnp.concatenate(outs, axis=0).T.astype(BF16)


def _mla_kernel(q_ref, k_ref, v_ref, o_ref, m_ref, l_ref, acc_ref, *, tq, tk):
    i = pl.program_id(1)
    sub = tq // tk
    _attend_init(m_ref, l_ref, acc_ref)

    def chunk(c, bias):
        off = pl.multiple_of(c * tk, tk)
        _attend_heads(lambda hd: k_ref[0, hd, pl.ds(off, tk), :],
                      lambda hd: q_ref[0, hd],
                      lambda hd: v_ref[0, hd, :, pl.ds(off, tk)],
                      bias, m_ref, l_ref, acc_ref, MLA_HEADS, MLA_V)

    def body(c, _):
        chunk(c, None)
        return 0

    lax.fori_loop(0, i * sub, body, 0)
    kpos = lax.broadcasted_iota(I32, (tk, tq), 0)
    qpos = lax.broadcasted_iota(I32, (tk, tq), 1)
    for j in range(sub):
        chunk(i * sub + j, jnp.where(j * tk + kpos <= qpos, 0.0, MASK_BIAS))
    _attend_finish(o_ref, l_ref, acc_ref, MLA_HEADS, MLA_V)


def _mla(q, k, v, tq, tk):
    B, H, S, _ = k.shape
    nq = S // tq
    return pl.pallas_call(
        functools.partial(_mla_kernel, tq=tq, tk=tk),
        grid=(B, nq),
        in_specs=[pl.BlockSpec((1, H, HEAD_PAD, tq), lambda b, i: (b, 0, 0, i)),
                  pl.BlockSpec((1, H, S, HEAD_PAD), lambda b, i: (b, 0, 0, 0)),
                  pl.BlockSpec((1, H, MLA_V, S), lambda b, i: (b, 0, 0, 0))],
        out_specs=pl.BlockSpec((tq, H * MLA_V), lambda b, i: (b * nq + i, 0)),
        out_shape=jax.ShapeDtypeStruct((B * S, H * MLA_V), BF16),
        scratch_shapes=[pltpu.VMEM((H, SUBLANES, tq), F32), pltpu.VMEM((H, SUBLANES, tq), F32),
                        pltpu.VMEM((H * MLA_V, tq), F32)],
        compiler_params=_cparams(("arbitrary", "arbitrary")),
        name="mla_attention",
    )(q, k, v)


def _float_key(score):
    bits = pltpu.bitcast(score + 0.0, I32)
    return jnp.where(bits >= 0, bits, bits ^ jnp.int32(0x7FFFFFFF))


def _dsa_kernel(qi_ref, kk_ref, iw_ref, qc_ref, vc_ref, o_ref,
                key_ref, cut_ref, m_ref, l_ref, acc_ref, *, tq, tk, topk, seq):
    i = pl.program_id(1)
    nch = (i + 1) * (tq // tk)
    q0 = i * tq
    row = lax.broadcasted_iota(I32, (tk, tq), 0)
    qpos = q0 + lax.broadcasted_iota(I32, (tk, tq), 1)
    HD = DSA_HEAD_DIM

    iw = iw_ref[0]

    def score_chunk(c, _):
        off = pl.multiple_of(c * tk, tk)
        kic = kk_ref[pl.ds(off, tk), :][:, :IDX_DIM]
        score = jnp.zeros((tk, tq), F32)
        for hd in range(IDX_HEADS):
            lg = jnp.dot(kic, qi_ref[0, hd * IDX_DIM:(hd + 1) * IDX_DIM, :], preferred_element_type=F32)
            score = score + jnp.maximum(lg, 0.0) * iw[hd:hd + 1, :]
        key_ref[pl.ds(off, tk), :] = jnp.where(off + row <= qpos, _float_key(score), INT_MIN)
        return 0

    lax.fori_loop(0, nch, score_chunk, 0)

    def count(pred):
        def body(c, part):
            off = pl.multiple_of(c * tk, tk)
            hit = jnp.where(pred(key_ref[pl.ds(off, tk), :], off), 1.0, 0.0)
            return part + jnp.sum(hit.reshape(tk // SUBLANES, SUBLANES, tq), axis=0)
        part = lax.fori_loop(0, nch, body, jnp.zeros((SUBLANES, tq), F32))
        return jnp.sum(part, axis=0, keepdims=True)

    def search(it, t):
        cand = t + (jnp.int32(1) << (31 - it))
        cnt = count(lambda kc, off: kc >= cand)
        return jnp.where(cnt >= topk, cand, t)

    thr = lax.fori_loop(0, 32, search, jnp.full((1, tq), INT_MIN, I32))

    n_gt = count(lambda kc, off: kc > thr)
    n_eq = count(lambda kc, off: kc == thr)
    need = topk - n_gt
    tie = (thr > INT_MIN) & (n_eq > need)
    cut_ref[...] = jnp.full(cut_ref.shape, seq, I32)

    @pl.when(jnp.max(jnp.where(tie, 1.0, 0.0)) > 0.0)
    def _():
        nbits = max(1, (seq - 1).bit_length())

        def find(it, p):
            cand = p + (jnp.int32(1) << (nbits - 1 - it))
            cnt = count(lambda kc, off: (kc == thr) & (off + row < cand))
            return jnp.where(cnt < need, cand, p)

        cut = lax.fori_loop(0, nbits, find, jnp.zeros((1, tq), I32))
        cut_ref[...] = jnp.broadcast_to(jnp.where(tie, cut, seq), cut_ref.shape)

    _attend_init(m_ref, l_ref, acc_ref)
    cut1 = cut_ref[0:1, :]

    def attend(c, _):
        off = pl.multiple_of(c * tk, tk)
        key = key_ref[pl.ds(off, tk), :]
        kpos = off + row
        sel = ((key > thr) | ((key == thr) & (kpos <= cut1))) & (kpos <= qpos)
        bias = jnp.where(sel, 0.0, MASK_BIAS)
        kc = kk_ref[pl.ds(off, tk), :][:, IDX_DIM:]
        vt = vc_ref[0, :, pl.ds(off, tk)]
        _attend_heads(lambda hd: kc, lambda hd: qc_ref[0, hd * HD:(hd + 1) * HD, :], lambda hd: vt,
                      bias, m_ref, l_ref, acc_ref, DSA_HEADS, HD)
        return 0

    lax.fori_loop(0, nch, attend, 0)
    _attend_finish(o_ref, l_ref, acc_ref, DSA_HEADS, HD)


def _dsa(qi, kk, iw, qc, vc, tq, tk):
    B, QR, S = qc.shape
    nq = S // tq
    topk = min(TOPK_MAX, S // 4)
    return pl.pallas_call(
        functools.partial(_dsa_kernel, tq=tq, tk=tk, topk=topk, seq=S),
        grid=(B, nq),
        in_specs=[pl.BlockSpec((1, QR, tq), lambda b, i: (b, 0, i)),
                  pl.BlockSpec((S, LANES), lambda b, i: (b, 0)),
                  pl.BlockSpec((1, IDX_HEADS, tq), lambda b, i: (b, 0, i)),
                  pl.BlockSpec((1, QR, tq), lambda b, i: (b, 0, i)),
                  pl.BlockSpec((1, DSA_HEAD_DIM, S), lambda b, i: (b, 0, 0))],
        out_specs=pl.BlockSpec((tq, QR), lambda b, i: (b * nq + i, 0)),
        out_shape=jax.ShapeDtypeStruct((B * S, QR), BF16),
        scratch_shapes=[pltpu.VMEM((S, tq), I32),
                        pltpu.VMEM((SUBLANES, tq), I32),
                        pltpu.VMEM((DSA_HEADS, SUBLANES, tq), F32),
                        pltpu.VMEM((DSA_HEADS, SUBLANES, tq), F32),
                        pltpu.VMEM((QR, tq), F32)],
        compiler_params=_cparams(("arbitrary", "arbitrary")),
        name="dsa_attention",
    )(qi, kk, iw, qc, vc)


def _merge_kernel(x_ref, sh1_ref, sc1_ref, g1_ref, sh2_ref, sc2_ref, n1_ref, n2_ref,
                  oa_ref, ob_ref, oc_ref, wg_ref, bg_ref, wb_ref, wo_ref, rw_ref, rb_ref,
                  xo_ref, h2_ref, idx_ref, gate_ref):
    x = x_ref[...]
    tm, D = x.shape
    ms = jnp.mean(x * x, axis=-1, keepdims=True)
    h = (x * lax.rsqrt(ms + NORM_EPS) * n1_ref[...]) * (1.0 + sc1_ref[0]) + sh1_ref[0]
    hb = h.astype(BF16)
    mix = jnp.zeros(x.shape, F32)
    for n, o_ref in enumerate((oa_ref, ob_ref, oc_ref)):
        y = jnp.dot(o_ref[...], wb_ref[n], preferred_element_type=F32)
        gl = jnp.dot(hb, wg_ref[:, n * D:(n + 1) * D], preferred_element_type=F32) + bg_ref[:, n * D:(n + 1) * D]
        mix = mix + jax.nn.sigmoid(gl) * y
    out = jnp.dot(mix.astype(BF16), wo_ref[...], preferred_element_type=F32)
    xn = x + g1_ref[0] * out
    xo_ref[...] = xn
    ms2 = jnp.mean(xn * xn, axis=-1, keepdims=True)
    h2 = (xn * lax.rsqrt(ms2 + NORM_EPS) * n2_ref[...]) * (1.0 + sc2_ref[0]) + sh2_ref[0]
    for s in range(D // LANES):
        h2_ref[pl.ds(s, tm, stride=D // LANES), :] = h2[:, s * LANES:(s + 1) * LANES]
    logits = lax.dot_general(rw_ref[...], h2.astype(BF16), NT, preferred_element_type=F32) + rb_ref[...]
    eidx = lax.broadcasted_iota(I32, logits.shape, 0)
    vals, idxs = [], []
    for _ in range(TOP_K):
        mx = jnp.max(logits, axis=0, keepdims=True)
        am = jnp.min(jnp.where(logits == mx, eidx, N_EXPERTS), axis=0, keepdims=True)
        vals.append(mx)
        idxs.append(am)
        logits = jnp.where(eidx == am, -jnp.inf, logits)
    ex = [jnp.exp(v - vals[0]) for v in vals]
    den = ex[0] + ex[1] + ex[2] + ex[3]
    pad_i = [jnp.zeros_like(idxs[0])] * (SUBLANES - TOP_K)
    pad_f = [jnp.zeros_like(den)] * (SUBLANES - TOP_K)
    idx_ref[...] = jnp.concatenate(idxs + pad_i, axis=0)
    gate_ref[...] = jnp.concatenate([e / den for e in ex] + pad_f, axis=0)


def _merge(x2, mods, n1, n2, oa, ob, oc, wts, B, S, tm):
    T, D = x2.shape
    tps = S // tm
    R = D // LANES
    row = lambda i: (i, 0)
    bat = lambda i: (i // tps, 0, 0)
    full2 = lambda i: (0, 0)
    full3 = lambda i: (0, 0, 0)
    mod_spec = pl.BlockSpec((1, 1, D), bat)
    W = BRANCH_WIDTH
    return pl.pallas_call(
        _merge_kernel,
        grid=(T // tm,),
        in_specs=[pl.BlockSpec((tm, D), row)] + [mod_spec] * 5 + [
            pl.BlockSpec((1, D), full2), pl.BlockSpec((1, D), full2),
            pl.BlockSpec((tm, W), row), pl.BlockSpec((tm, W), row), pl.BlockSpec((tm, W), row),
            pl.BlockSpec((D, N_BRANCH * D), full2), pl.BlockSpec((1, N_BRANCH * D), full2),
            pl.BlockSpec((N_BRANCH, W, D), full3), pl.BlockSpec((D, D), full2),
            pl.BlockSpec((N_EXPERTS, D), full2), pl.BlockSpec((N_EXPERTS, 1), full2)],
        out_specs=[pl.BlockSpec((tm, D), row), pl.BlockSpec((tm * R, LANES), row),
                   pl.BlockSpec((SUBLANES, tm), lambda i: (0, i)), pl.BlockSpec((SUBLANES, tm), lambda i: (0, i))],
        out_shape=[jax.ShapeDtypeStruct((T, D), F32), jax.ShapeDtypeStruct((T * R, LANES), F32),
                   jax.ShapeDtypeStruct((SUBLANES, T), I32), jax.ShapeDtypeStruct((SUBLANES, T), F32)],
        compiler_params=_cparams(("arbitrary",)),
        name="merge_router",
    )(x2, *mods, n1, n2, oa, ob, oc, wts["w_gate"], wts["b_gate"], wts["w_branch"], wts["w_out"],
      wts["router_wt"], wts["router_b"])


def _deint_kernel(w_ref, p_ref, o_ref):
    o_ref[0] = jnp.dot(w_ref[0, 0].astype(BF16), p_ref[...], preferred_element_type=F32).astype(BF16)


def _deinterleave_w1(exp_w1, l):
    _, E, D, F2 = exp_w1.shape
    half = DEINT // 2
    j = jnp.arange(half)
    perm = jnp.zeros((DEINT, DEINT), BF16).at[2 * j, j].set(1.0).at[2 * j + 1, half + j].set(1.0)
    return pl.pallas_call(
        _deint_kernel,
        grid=(E, F2 // DEINT),
        in_specs=[pl.BlockSpec((1, 1, D, DEINT), lambda e, b: (l, e, 0, b)),
                  pl.BlockSpec((DEINT, DEINT), lambda e, b: (0, 0))],
        out_specs=pl.BlockSpec((1, D, DEINT), lambda e, b: (e, 0, b)),
        out_shape=jax.ShapeDtypeStruct((E, D, F2), BF16),
        compiler_params=_cparams(("arbitrary", "arbitrary")),
        name="w1_deinterleave",
    )(exp_w1, perm)


def _moe_kernel(be_ref, nu_ref, tokc_ref, tokn_ref, dst_ref, x_hbm, w1_ref, w2_ref, b1_ref, b2_ref,
                y_hbm, xbuf, ybuf, gsem, ssem, zsem, *, tb, hchunk):
    i = pl.program_id(0)
    nused = nu_ref[0]
    slot = i % 2
    R = SUBLANES

    def gather(tok_ref, s):
        def body(r, _):
            src = pl.multiple_of(tok_ref[0, 0, r], R)
            pltpu.make_async_copy(x_hbm.at[pl.ds(src, R)], xbuf.at[s, pl.ds(pl.multiple_of(r * R, R), R)],
                                  gsem.at[s]).start()
            return 0
        lax.fori_loop(0, tb, body, 0, unroll=8)

    def wait_scatter(s):
        pltpu.make_async_copy(ybuf.at[s], y_hbm.at[pl.ds(0, tb * R)], ssem.at[s]).wait()

    @pl.when(i == 0)
    def _():
        gather(tokc_ref, 0)

    @pl.when(i + 1 < nused)
    def _():
        gather(tokn_ref, 1 - slot)

    @pl.when(i < nused)
    def _():
        pltpu.make_async_copy(x_hbm.at[pl.ds(0, tb * R)], xbuf.at[slot], gsem.at[slot]).wait()

        @pl.when(i >= 2)
        def _():
            wait_scatter(slot)

        D = R * LANES
        xb = jnp.concatenate([xbuf[slot, pl.ds(s, tb, stride=R), :] for s in range(R)], axis=1).astype(BF16)
        y = jnp.zeros((tb, D), F32) + b2_ref[0]
        q = DEINT // 2
        for j in range(D // hchunk):
            cs = slice(2 * j * hchunk, 2 * (j + 1) * hchunk)
            hgu = jnp.dot(xb, w1_ref[0, :, cs], preferred_element_type=F32) + b1_ref[0, :, cs]
            nb = 2 * hchunk // DEINT
            hg = jnp.concatenate([hgu[:, b * DEINT:b * DEINT + q] for b in range(nb)], axis=1)
            hu = jnp.concatenate([hgu[:, b * DEINT + q:(b + 1) * DEINT] for b in range(nb)], axis=1)
            gate = jnp.minimum(hg, SWIGLU_LIMIT)
            up = jnp.clip(hu, -SWIGLU_LIMIT, SWIGLU_LIMIT)
            act = (up + 1.0) * (gate * jax.nn.sigmoid(gate * SWIGLU_ALPHA))
            y = y + jnp.dot(act.astype(BF16), w2_ref[0, j * hchunk:(j + 1) * hchunk, :],
                            preferred_element_type=F32)
        for s in range(R):
            ybuf[slot, pl.ds(s, tb, stride=R), :] = y[:, s * LANES:(s + 1) * LANES]

        def scatter(r, _):
            dst = pl.multiple_of(dst_ref[0, 0, r], R)
            pltpu.make_async_copy(ybuf.at[slot, pl.ds(pl.multiple_of(r * R, R), R)], y_hbm.at[pl.ds(dst, R)],
                                  ssem.at[slot]).start()
            return 0
        lax.fori_loop(0, tb, scatter, 0, unroll=8)

    @pl.when(i == nused - 1)
    def _():
        wait_scatter(slot)

        @pl.when(i >= 1)
        def _():
            wait_scatter(1 - slot)

    @pl.when(i >= nused)
    def _():
        xbuf[slot] = jnp.zeros(xbuf.shape[1:], F32)
        rows = pl.ds(pl.multiple_of(i * (tb * R), tb * R), tb * R)
        fill = pltpu.make_async_copy(xbuf.at[slot], y_hbm.at[rows], zsem)
        fill.start()
        fill.wait()


def _moe(h2t, block_expert, nused, tok_pad, dst_pad, wts, tb):
    R = SUBLANES
    D = R * LANES
    P = tok_pad.shape[0]
    nblk = P // tb
    tok3 = (tok_pad * R).reshape(nblk, 1, tb)
    dst3 = (dst_pad * R).reshape(nblk, 1, tb)
    smem_blk = lambda f: pl.BlockSpec((1, 1, tb), f, memory_space=pltpu.SMEM)
    F2 = wts["w1"].shape[2]
    grid_spec = pltpu.PrefetchScalarGridSpec(
        num_scalar_prefetch=2,
        grid=(nblk,),
        in_specs=[smem_blk(lambda i, be, nu: (i, 0, 0)),
                  smem_blk(lambda i, be, nu: (jnp.minimum(i + 1, nblk - 1), 0, 0)),
                  smem_blk(lambda i, be, nu: (i, 0, 0)),
                  pl.BlockSpec(memory_space=pl.ANY),
                  pl.BlockSpec((1, D, F2), lambda i, be, nu: (be[i], 0, 0)),
                  pl.BlockSpec((1, F2 // 2, D), lambda i, be, nu: (be[i], 0, 0)),
                  pl.BlockSpec((1, 1, F2), lambda i, be, nu: (be[i], 0, 0)),
                  pl.BlockSpec((1, 1, D), lambda i, be, nu: (be[i], 0, 0))],
        out_specs=pl.BlockSpec(memory_space=pl.ANY),
        scratch_shapes=[pltpu.VMEM((2, tb * R, LANES), F32), pltpu.VMEM((2, tb * R, LANES), F32),
                        pltpu.SemaphoreType.DMA((2,)), pltpu.SemaphoreType.DMA((2,)), pltpu.SemaphoreType.DMA],
    )
    return pl.pallas_call(
        functools.partial(_moe_kernel, tb=tb, hchunk=256),
        grid_spec=grid_spec,
        out_shape=jax.ShapeDtypeStruct((P * R, LANES), F32),
        compiler_params=_cparams(("arbitrary",)),
        name="moe_ffn",
    )(block_expert, nused, tok3, tok3, dst3, h2t, wts["w1"], wts["w2"], wts["b1"], wts["b2"])


def _route(top_idx, T, tb):
    E = N_EXPERTS
    N = T * TOP_K
    nblk = (N + E * (tb - 1) + tb - 1) // tb
    P = nblk * tb
    flat_e = top_idx.reshape(N)
    order = jnp.argsort(flat_e).astype(I32)
    e_sorted = flat_e[order]
    counts = jnp.bincount(flat_e, length=E).astype(I32)
    padded = (counts + tb - 1) // tb * tb
    starts = jnp.cumsum(counts) - counts
    pends = jnp.cumsum(padded)
    pstarts = pends - padded
    dest = pstarts[e_sorted] + (jnp.arange(N, dtype=I32) - starts[e_sorted])
    slot_at = jnp.full((P,), -1, I32).at[dest].set(order)
    is_pad = slot_at < 0
    pad_rank = jnp.cumsum(is_pad.astype(I32)) - 1
    tok_pad = jnp.where(is_pad, 0, slot_at % T)
    dst_pad = jnp.where(is_pad, N + pad_rank, slot_at)
    nused = (pends[E - 1] // tb).astype(I32)
    blk = jnp.arange(nblk, dtype=I32)
    block_expert = jnp.minimum(jnp.searchsorted(pends, jnp.minimum(blk, nused - 1) * tb, side='right'),
                               E - 1).astype(I32)
    return block_expert, nused.reshape(1), tok_pad, dst_pad


def _combine_kernel(x_ref, g2_ref, gate_ref, y0_ref, y1_ref, y2_ref, y3_ref, fg_ref, o_ref, *, final):
    g = gate_ref[...]
    tm, D = x_ref.shape
    R = D // LANES
    cols = []
    for s in range(R):
        rows = pl.ds(s, tm, stride=R)
        cols.append(g[:, 0:1] * y0_ref[rows, :] + g[:, 1:2] * y1_ref[rows, :]
                    + g[:, 2:3] * y2_ref[rows, :] + g[:, 3:4] * y3_ref[rows, :])
    x = x_ref[...] + g2_ref[0] * jnp.concatenate(cols, axis=1)
    if final:
        x = x * lax.rsqrt(jnp.mean(x * x, axis=-1, keepdims=True) + NORM_EPS) * fg_ref[...]
    o_ref[...] = x


def _combine(x2, g2, gates, y4, final_g, B, S, tm, final):
    T, D = x2.shape
    R = D // LANES
    tps = S // tm
    nt = T // tm
    row = lambda i: (i, 0)
    yspec = lambda k: pl.BlockSpec((tm * R, LANES), lambda i, k=k: (k * nt + i, 0))
    return pl.pallas_call(
        functools.partial(_combine_kernel, final=final),
        grid=(nt,),
        in_specs=[pl.BlockSpec((tm, D), row),
                  pl.BlockSpec((1, 1, D), lambda i: (i // tps, 0, 0)),
                  pl.BlockSpec((tm, TOP_K), row),
                  yspec(0), yspec(1), yspec(2), yspec(3),
                  pl.BlockSpec((1, D), lambda i: (0, 0))],
        out_specs=pl.BlockSpec((tm, D), row),
        out_shape=jax.ShapeDtypeStruct((T, D), F32),
        compiler_params=_cparams(("arbitrary",)),
        name="combine",
    )(x2, g2, gates, y4, y4, y4, y4, final_g)


def _pack_layer(l, w_in, b_gate, mla_q_norm, mla_w_uq, mla_kv_norm, mla_w_ukv, conv_w, w_branch, w_out,
                router_w, router_b, exp_w1, exp_b1, exp_w2, exp_b2):
    D = w_in.shape[1]
    wi = w_in[l]
    o = [0]

    def seg(n):
        s = wi[:, o[0]:o[0] + n]
        o[0] += n
        return s
    q_lat, kv_full, conv = seg(MLA_Q_LORA), seg(MLA_KV_LORA + MLA_ROPE), seg(3 * CONV_DIM)
    dq, dkv, iq = seg(DSA_HEADS * DSA_HEAD_DIM), seg(2 * DSA_HEAD_DIM), seg(IDX_HEADS * IDX_DIM)
    ik, iw, gates = seg(IDX_DIM), seg(IDX_HEADS), seg(N_BRANCH * D)
    zeros = lambda n: jnp.zeros((D, n), wi.dtype)
    w_main = jnp.concatenate([
        q_lat, kv_full[:, :MLA_KV_LORA],
        kv_full[:, MLA_KV_LORA:], zeros(LANES - MLA_ROPE),
        ik, dkv[:, :DSA_HEAD_DIM],
        conv], axis=1).astype(BF16)
    assert w_main.shape[1] == MAIN_COLS
    w_t = jnp.concatenate([dq, iq, dkv[:, DSA_HEAD_DIM:], iw, zeros(T_ROWS - ROW_IW - IDX_HEADS)],
                          axis=1).T.astype(BF16)
    assert w_t.shape[0] == T_ROWS
    H = MLA_HEADS
    hq = MLA_ROPE + MLA_NOPE
    w_uq = mla_w_uq[l].reshape(MLA_Q_LORA, H, hq)
    w_uq = jnp.pad(w_uq, ((0, 0), (0, 0), (0, HEAD_PAD - hq))).transpose(1, 2, 0).astype(BF16)
    w_ukv = mla_w_ukv[l].reshape(MLA_KV_LORA, H, MLA_NOPE + MLA_V)
    w_k = jnp.pad(w_ukv[:, :, :MLA_NOPE], ((0, 0), (0, 0), (MLA_ROPE, HEAD_PAD - hq)))
    w_k = w_k.transpose(1, 0, 2).astype(BF16)
    w_v = w_ukv[:, :, MLA_NOPE:].transpose(1, 2, 0).astype(BF16)
    b1 = exp_b1[l]
    E, F2 = b1.shape
    b1p = b1.reshape(E, F2 // DEINT, DEINT // 2, 2).transpose(0, 1, 3, 2).reshape(E, 1, F2)
    return dict(
        w_main=w_main, w_t=w_t,
        q_norm=mla_q_norm[l].reshape(1, -1), kv_norm=mla_kv_norm[l].reshape(1, -1),
        w_uq=w_uq, w_k=w_k, w_v=w_v, conv_w=conv_w[l],
        w_gate=gates.astype(BF16), b_gate=b_gate[l].reshape(1, -1),
        w_branch=w_branch[l].astype(BF16), w_out=w_out[l].astype(BF16),
        router_wt=router_w[l].T.astype(BF16), router_b=router_b[l].reshape(-1, 1),
        w1=_deinterleave_w1(exp_w1, l), w2=exp_w2[l].astype(BF16),
        b1=b1p, b2=exp_b2[l].reshape(E, 1, -1),
    )


def _tiles(S, T):
    tm = min(512, S)
    tq = min(256, S)
    tk = min(256, tq)
    N = T * TOP_K
    tb = 512 if N >= 512 * N_EXPERTS else 128
    return tm, tq, tk, tb


def kernel(x, c, positions, norm1_g, norm2_g, w_ada, b_ada, w_in, b_gate, mla_q_norm, mla_w_uq, mla_kv_norm,
           mla_w_ukv, conv_w, w_branch, w_out, router_w, router_b, exp_w1, exp_b1, exp_w2, exp_b2, final_g):
    B, S, D = x.shape
    T = B * S
    L = w_ada.shape[0]
    tm, tq, tk, tb = _tiles(S, T)
    mod = _ada(c, w_ada, b_ada)
    tabs = _rope_tables(positions)
    x2 = x.reshape(T, D)
    for l in range(L):
        wts = _pack_layer(l, w_in, b_gate, mla_q_norm, mla_w_uq, mla_kv_norm, mla_w_ukv, conv_w, w_branch,
                          w_out, router_w, router_b, exp_w1, exp_b1, exp_w2, exp_b2)
        sh1, sc1, g1, sh2, sc2, g2 = [m.reshape(B, 1, D) for m in jnp.split(mod[l], 6, axis=-1)]
        n1 = norm1_g[l].reshape(1, D)
        n2 = norm2_g[l].reshape(1, D)
        q, k, v, ob, qc, qi, kk, vc, iw = _front(x2, sh1, sc1, n1, wts, tabs, B, S, tm)
        oa = _mla(q, k, v, tq, tk)
        oc = _dsa(qi, kk, iw, qc, vc, tq, tk)
        x2, h2t, idx8, gate8 = _merge(x2, (sh1, sc1, g1, sh2, sc2), n1, n2, oa, ob, oc, wts, B, S, tm)
        block_expert, nused, tok_pad, dst_pad = _route(idx8[:TOP_K], T, tb)
        y4 = _moe(h2t, block_expert, nused, tok_pad, dst_pad, wts, tb)
        x2 = _combine(x2, g2, gate8[:TOP_K].T, y4, final_g.reshape(1, D), B, S, tm, final=(l == L - 1))
    return x2.reshape(B, S, D)
```

```python
import functools
import math

import jax
import jax.numpy as jnp
from jax import lax
from jax.experimental import pallas as pl
from jax.experimental.pallas import tpu as pltpu

F32 = jnp.float32
BF16 = jnp.bfloat16
I32 = jnp.int32

ROPE_THETA = 500000.0
NORM_EPS = 1e-6

MLA_HEADS = 8
MLA_NOPE = 64
MLA_ROPE = 32
MLA_V = 64
MLA_Q_LORA = 256
MLA_KV_LORA = 128
CONV_DIM = 512
CONV_WIDTH = 3
DSA_HEADS = 8
DSA_HEAD_DIM = 64
DSA_ROT = 16
IDX_HEADS = 8
IDX_DIM = 64
IDX_ROT = 16
TOPK_MAX = 256
N_BRANCH = 3
BRANCH_WIDTH = 512
N_EXPERTS = 32
TOP_K = 4
SWIGLU_ALPHA = 1.702
SWIGLU_LIMIT = 7.0

LANES = 128
SUBLANES = 8
HEAD_PAD = 128
INT_MIN = -(2 ** 31)
MASK_BIAS = -1e30
LOG2E = math.log2(math.e)
VMEM_LIMIT = 56 * 1024 * 1024
NT = (((1,), (1,)), ((), ()))

SEG_QLAT = 0
SEG_KVLAT = 256
SEG_KPE = 384
SEG_KK = 512
SEG_CONV = 640
MAIN_COLS = 2176
ROW_DQ = 0
ROW_IQ = 512
ROW_VC = 1024
ROW_IW = 1088
T_ROWS = 1104
DEINT = 256


def _cparams(sem):
    return pltpu.CompilerParams(dimension_semantics=sem, vmem_limit_bytes=VMEM_LIMIT)


def _rope(x, cos, sin, half, period):
    lane = lax.broadcasted_iota(I32, x.shape, 1)
    first = (lane % period) < half
    rot = jnp.where(first, pltpu.roll(x, LANES - half, axis=1), pltpu.roll(x, half, axis=1))
    return x * cos + rot * sin


def _rope_t(x, cos, sin, half, period):
    R = x.shape[0]
    row = lax.broadcasted_iota(I32, x.shape, 0)
    first = (row % period) < half
    rot = jnp.where(first, pltpu.roll(x, R - half, axis=0), pltpu.roll(x, half, axis=0))
    reps = R // period
    if reps > 1:
        cos = jnp.tile(cos, (reps, 1))
        sin = jnp.tile(sin, (reps, 1))
    return x * cos + rot * sin


def _ada_kernel(c_ref, w_ref, b_ref, o_ref):
    c = c_ref[...]
    act = c * jax.nn.sigmoid(c)
    o_ref[0] = jnp.dot(act, w_ref[0], preferred_element_type=F32,
                       precision=lax.Precision.HIGHEST) + b_ref[0]


def _ada(c, w_ada, b_ada):
    L, D, D6 = w_ada.shape
    B = c.shape[0]
    nblk = D6 // D
    return pl.pallas_call(
        _ada_kernel,
        grid=(L, nblk),
        in_specs=[pl.BlockSpec((B, D), lambda l, j: (0, 0)),
                  pl.BlockSpec((1, D, D), lambda l, j: (l, 0, j)),
                  pl.BlockSpec((1, 1, D), lambda l, j: (l, 0, j))],
        out_specs=pl.BlockSpec((1, B, D), lambda l, j: (l, 0, j)),
        out_shape=jax.ShapeDtypeStruct((L, B, D6), F32),
        compiler_params=_cparams(("arbitrary", "arbitrary")),
        name="adaln",
    )(c, w_ada, b_ada.reshape(L, 1, D6))


def _trig_kernel(freq_ref, pos_ref, cos_ref, sin_ref):
    ang = pos_ref[...].astype(F32) * freq_ref[pl.program_id(0)]
    cos_ref[0] = jnp.cos(ang)
    sin_ref[0] = jnp.sin(ang)


def _rope_tables(positions):
    B, S = positions.shape
    T = B * S
    ha, hc = MLA_ROPE // 2, DSA_ROT // 2
    fa = jnp.exp(-math.log(ROPE_THETA) * jnp.arange(ha, dtype=F32) * (2.0 / MLA_ROPE))
    fc = jnp.exp(-math.log(ROPE_THETA) * jnp.arange(hc, dtype=F32) * (2.0 / DSA_ROT))
    freqs = jnp.concatenate([fa, fc])
    nf = ha + hc
    cos, sin = pl.pallas_call(
        _trig_kernel,
        grid=(nf,),
        in_specs=[pl.BlockSpec(memory_space=pltpu.SMEM),
                  pl.BlockSpec((B, S), lambda i: (0, 0))],
        out_specs=[pl.BlockSpec((1, B, S), lambda i: (i, 0, 0)),
                   pl.BlockSpec((1, B, S), lambda i: (i, 0, 0))],
        out_shape=[jax.ShapeDtypeStruct((nf, B, S), F32)] * 2,
        compiler_params=_cparams(("arbitrary",)),
        name="rope_trig",
    )(freqs, positions)
    cos = cos.reshape(nf, T)
    sin = sin.reshape(nf, T)
    ca, sa, cc, sc = cos[:ha], sin[:ha], cos[ha:], sin[ha:]
    cos_at = jnp.concatenate([ca, ca, jnp.ones((HEAD_PAD - 2 * ha, T), F32)], axis=0)
    sin_at = jnp.concatenate([-sa, sa, jnp.zeros((HEAD_PAD - 2 * ha, T), F32)], axis=0)
    cos_ct = jnp.concatenate([cc, cc, jnp.ones((DSA_HEAD_DIM - 2 * hc, T), F32)], axis=0)
    sin_ct = jnp.concatenate([-sc, sc, jnp.zeros((DSA_HEAD_DIM - 2 * hc, T), F32)], axis=0)
    cos_a, sin_a = cos_at.T, sin_at.T
    cos_c = jnp.concatenate([cos_ct, cos_ct], axis=0).T
    sin_c = jnp.concatenate([sin_ct, sin_ct], axis=0).T
    return dict(cos_a=cos_a, sin_a=sin_a, cos_c=cos_c, sin_c=sin_c,
                cos_at=cos_at, sin_at=sin_at, cos_ct=cos_ct, sin_ct=sin_ct)


def _front_kernel(x_ref, sh_ref, sc_ref, g_ref, w_ref, wt_ref, qn_ref, wuq_ref, kvn_ref, wk_ref, wv_ref, cw_ref,
                  ca_ref, sa_ref, cc_ref, sn_ref, cat_ref, sat_ref, cct_ref, snt_ref,
                  q_out, k_out, v_out, ob_out, qc_out, qi_out, kk_out, vc_out, iw_out,
                  carry_ref, *, tiles_per_seq):
    i = pl.program_id(0)
    x = x_ref[...]
    ms = jnp.mean(x * x, axis=-1, keepdims=True)
    h = (x * lax.rsqrt(ms + NORM_EPS) * g_ref[...]) * (1.0 + sc_ref[0]) + sh_ref[0]
    hb = h.astype(BF16)

    def proj(start, width):
        return jnp.dot(hb, w_ref[:, start:start + width], preferred_element_type=F32)

    def proj_t(start, rows):
        return lax.dot_general(wt_ref[start:start + rows, :], hb, NT, preferred_element_type=F32)

    q_lat = proj(SEG_QLAT, MLA_Q_LORA)
    cq = q_lat * lax.rsqrt(jnp.mean(q_lat * q_lat, axis=-1, keepdims=True) + NORM_EPS) * qn_ref[...]
    cqb = cq.astype(BF16)
    kv_lat = proj(SEG_KVLAT, MLA_KV_LORA)
    ckv = kv_lat * lax.rsqrt(jnp.mean(kv_lat * kv_lat, axis=-1, keepdims=True) + NORM_EPS) * kvn_ref[...]
    ckvb = ckv.astype(BF16)
    k_pe = _rope(proj(SEG_KPE, LANES), ca_ref[...], sa_ref[...], MLA_ROPE // 2, LANES)
    cat, sat = cat_ref[...], sat_ref[...]
    q_scale = (MLA_NOPE + MLA_ROPE) ** -0.5 * LOG2E
    for hd in range(MLA_HEADS):
        qh = lax.dot_general(wuq_ref[hd], cqb, NT, preferred_element_type=F32)
        q_out[0, hd] = (_rope_t(qh, cat, sat, MLA_ROPE // 2, HEAD_PAD) * q_scale).astype(BF16)
        kh = jnp.dot(ckvb, wk_ref[hd], preferred_element_type=F32) + k_pe
        k_out[0, hd] = kh.astype(BF16)
        v_out[0, hd] = lax.dot_general(wv_ref[hd], ckvb, NT, preferred_element_type=F32).astype(BF16)

    g_b = proj(SEG_CONV, CONV_DIM)
    u = proj(SEG_CONV + CONV_DIM, CONV_DIM) * proj(SEG_CONV + 2 * CONV_DIM, CONV_DIM)
    tm = u.shape[0]

    @pl.when(i % tiles_per_seq == 0)
    def _():
        carry_ref[...] = jnp.zeros(carry_ref.shape, F32)

    prev = carry_ref[...]
    row = lax.broadcasted_iota(I32, u.shape, 0)
    u1 = jnp.where(row == 0, prev[7:8], pltpu.roll(u, 1, axis=0))
    u2 = jnp.where(row == 0, prev[6:7], jnp.where(row == 1, prev[7:8], pltpu.roll(u, 2, axis=0)))
    carry_ref[...] = u[tm - SUBLANES:tm]
    cw = cw_ref[...]
    y = cw[0:1] * u2 + cw[1:2] * u1 + cw[2:3] * u
    ob_out[...] = (g_b * y).astype(BF16)

    hc = DSA_ROT // 2
    cct, snt = cct_ref[...], snt_ref[...]
    nq_rows = DSA_HEADS * DSA_HEAD_DIM
    qc_scale = DSA_HEAD_DIM ** -0.5 * LOG2E
    qc_out[0] = (_rope_t(proj_t(ROW_DQ, nq_rows), cct, snt, hc, DSA_HEAD_DIM) * qc_scale).astype(BF16)
    qi_out[0] = (_rope_t(proj_t(ROW_IQ, nq_rows), cct, snt, hc, IDX_DIM) * IDX_DIM ** -0.5).astype(BF16)
    kk_out[...] = _rope(proj(SEG_KK, LANES), cc_ref[...], sn_ref[...], hc, DSA_HEAD_DIM).astype(BF16)
    vc_out[0] = proj_t(ROW_VC, DSA_HEAD_DIM).astype(BF16)
    iw_out[0] = proj_t(ROW_IW, IDX_HEADS) * IDX_HEADS ** -0.5


def _front(x2, shift, scale, norm_g, wts, tabs, B, S, tm):
    T, D = x2.shape
    tps = S // tm
    nt = T // tm
    row = lambda i: (i, 0)
    col = lambda i: (0, i)
    bat = lambda i: (i // tps, 0, 0)
    head = lambda i: (i // tps, 0, i % tps, 0)
    head_t = lambda i: (i // tps, 0, 0, i % tps)
    seq_t = lambda i: (i // tps, 0, i % tps)
    full2 = lambda i: (0, 0)
    full3 = lambda i: (0, 0, 0)
    H = MLA_HEADS
    QR = DSA_HEADS * DSA_HEAD_DIM
    in_specs = [
        pl.BlockSpec((tm, D), row),
        pl.BlockSpec((1, 1, D), bat),
        pl.BlockSpec((1, 1, D), bat),
        pl.BlockSpec((1, D), full2),
        pl.BlockSpec((D, MAIN_COLS), full2),
        pl.BlockSpec((T_ROWS, D), full2),
        pl.BlockSpec((1, MLA_Q_LORA), full2),
        pl.BlockSpec((H, HEAD_PAD, MLA_Q_LORA), full3),
        pl.BlockSpec((1, MLA_KV_LORA), full2),
        pl.BlockSpec((H, MLA_KV_LORA, HEAD_PAD), full3),
        pl.BlockSpec((H, MLA_V, MLA_KV_LORA), full3),
        pl.BlockSpec((CONV_WIDTH, CONV_DIM), full2),
        pl.BlockSpec((tm, LANES), row),
        pl.BlockSpec((tm, LANES), row),
        pl.BlockSpec((tm, LANES), row),
        pl.BlockSpec((tm, LANES), row),
        pl.BlockSpec((HEAD_PAD, tm), col),
        pl.BlockSpec((HEAD_PAD, tm), col),
        pl.BlockSpec((DSA_HEAD_DIM, tm), col),
        pl.BlockSpec((DSA_HEAD_DIM, tm), col),
    ]
    out_specs = [
        pl.BlockSpec((1, H, HEAD_PAD, tm), head_t),
        pl.BlockSpec((1, H, tm, HEAD_PAD), head),
        pl.BlockSpec((1, H, MLA_V, tm), head_t),
        pl.BlockSpec((tm, CONV_DIM), row),
        pl.BlockSpec((1, QR, tm), seq_t),
        pl.BlockSpec((1, QR, tm), seq_t),
        pl.BlockSpec((tm, LANES), row),
        pl.BlockSpec((1, DSA_HEAD_DIM, tm), seq_t),
        pl.BlockSpec((1, IDX_HEADS, tm), seq_t),
    ]
    out_shape = [
        jax.ShapeDtypeStruct((B, H, HEAD_PAD, S), BF16),
        jax.ShapeDtypeStruct((B, H, S, HEAD_PAD), BF16),
        jax.ShapeDtypeStruct((B, H, MLA_V, S), BF16),
        jax.ShapeDtypeStruct((T, CONV_DIM), BF16),
        jax.ShapeDtypeStruct((B, QR, S), BF16),
        jax.ShapeDtypeStruct((B, QR, S), BF16),
        jax.ShapeDtypeStruct((T, LANES), BF16),
        jax.ShapeDtypeStruct((B, DSA_HEAD_DIM, S), BF16),
        jax.ShapeDtypeStruct((B, IDX_HEADS, S), F32),
    ]
    return pl.pallas_call(
        functools.partial(_front_kernel, tiles_per_seq=tps),
        grid=(nt,),
        in_specs=in_specs,
        out_specs=out_specs,
        out_shape=out_shape,
        scratch_shapes=[pltpu.VMEM((SUBLANES, CONV_DIM), F32)],
        compiler_params=_cparams(("arbitrary",)),
        name="front",
    )(x2, shift, scale, norm_g, wts["w_main"], wts["w_t"], wts["q_norm"], wts["w_uq"], wts["kv_norm"],
      wts["w_k"], wts["w_v"], wts["conv_w"],
      tabs["cos_a"], tabs["sin_a"], tabs["cos_c"], tabs["sin_c"],
      tabs["cos_at"], tabs["sin_at"], tabs["cos_ct"], tabs["sin_ct"])


def _attend_heads(key_tile, q_tile, v_tile, bias, m_ref, l_ref, acc_ref, nheads, vdim):
    scores = [jnp.dot(key_tile(hd), q_tile(hd), preferred_element_type=F32) for hd in range(nheads)]
    probs, alphas = [], []
    for hd in range(nheads):
        s = scores[hd] if bias is None else scores[hd] + bias
        m_new = jnp.maximum(m_ref[hd], jnp.max(s, axis=0, keepdims=True))
        alpha = jnp.exp2(m_ref[hd] - m_new)
        p = jnp.exp2(s - m_new[0:1])
        l_ref[hd] = alpha * l_ref[hd] + jnp.sum(p, axis=0, keepdims=True)
        m_ref[hd] = m_new
        probs.append(p.astype(BF16))
        alphas.append(alpha[0:1])
    for hd in range(nheads):
        rows = slice(hd * vdim, (hd + 1) * vdim)
        acc_ref[rows, :] = alphas[hd] * acc_ref[rows, :] + jnp.dot(v_tile(hd), probs[hd],
                                                                   preferred_element_type=F32)


def _attend_init(m_ref, l_ref, acc_ref):
    m_ref[...] = jnp.full(m_ref.shape, MASK_BIAS, F32)
    l_ref[...] = jnp.zeros(l_ref.shape, F32)
    acc_ref[...] = jnp.zeros(acc_ref.shape, F32)


def _attend_finish(o_ref, l_ref, acc_ref, nheads, vdim):
    outs = [acc_ref[hd * vdim:(hd + 1) * vdim, :] / l_ref[hd, 0:1, :] for hd in range(nheads)]
    o_ref[...] = jnp.concatenate(outs, axis=0).T.astype(BF16)


def _mla_kernel(q_ref, k_ref, v_ref, o_ref, m_ref, l_ref, acc_ref, *, tq, tk):
    i = pl.program_id(1)
    sub = tq // tk
    _attend_init(m_ref, l_ref, acc_ref)

    def chunk(c, bias):
        off = pl.multiple_of(c * tk, tk)
        _attend_heads(lambda hd: k_ref[0, hd, pl.ds(off, tk), :],
                      lambda hd: q_ref[0, hd],
                      lambda hd: v_ref[0, hd, :, pl.ds(off, tk)],
                      bias, m_ref, l_ref, acc_ref, MLA_HEADS, MLA_V)

    def body(c, _):
        chunk(c, None)
        return 0

    lax.fori_loop(0, i * sub, body, 0)
    kpos = lax.broadcasted_iota(I32, (tk, tq), 0)
    qpos = lax.broadcasted_iota(I32, (tk, tq), 1)
    for j in range(sub):
        chunk(i * sub + j, jnp.where(j * tk + kpos <= qpos, 0.0, MASK_BIAS))
    _attend_finish(o_ref, l_ref, acc_ref, MLA_HEADS, MLA_V)


def _mla(q, k, v, tq, tk):
    B, H, S, _ = k.shape
    nq = S // tq
    return pl.pallas_call(
        functools.partial(_mla_kernel, tq=tq, tk=tk),
        grid=(B, nq),
        in_specs=[pl.BlockSpec((1, H, HEAD_PAD, tq), lambda b, i: (b, 0, 0, i)),
                  pl.BlockSpec((1, H, S, HEAD_PAD), lambda b, i: (b, 0, 0, 0)),
                  pl.BlockSpec((1, H, MLA_V, S), lambda b, i: (b, 0, 0, 0))],
        out_specs=pl.BlockSpec((tq, H * MLA_V), lambda b, i: (b * nq + i, 0)),
        out_shape=jax.ShapeDtypeStruct((B * S, H * MLA_V), BF16),
        scratch_shapes=[pltpu.VMEM((H, SUBLANES, tq), F32), pltpu.VMEM((H, SUBLANES, tq), F32),
                        pltpu.VMEM((H * MLA_V, tq), F32)],
        compiler_params=_cparams(("arbitrary", "arbitrary")),
        name="mla_attention",
    )(q, k, v)


def _float_key(score):
    bits = pltpu.bitcast(score + 0.0, I32)
    return jnp.where(bits >= 0, bits, bits ^ jnp.int32(0x7FFFFFFF))


def _dsa_kernel(qi_ref, kk_ref, iw_ref, qc_ref, vc_ref, o_ref,
                key_ref, cut_ref, m_ref, l_ref, acc_ref, *, tq, tk, topk, seq):
    i = pl.program_id(1)
    nch = (i + 1) * (tq // tk)
    q0 = i * tq
    row = lax.broadcasted_iota(I32, (tk, tq), 0)
    qpos = q0 + lax.broadcasted_iota(I32, (tk, tq), 1)
    HD = DSA_HEAD_DIM

    iw = iw_ref[0]

    def score_chunk(c, _):
        off = pl.multiple_of(c * tk, tk)
        kic = kk_ref[pl.ds(off, tk), :][:, :IDX_DIM]
        score = jnp.zeros((tk, tq), F32)
        for hd in range(IDX_HEADS):
            lg = jnp.dot(kic, qi_ref[0, hd * IDX_DIM:(hd + 1) * IDX_DIM, :], preferred_element_type=F32)
            score = score + jnp.maximum(lg, 0.0) * iw[hd:hd + 1, :]
        key_ref[pl.ds(off, tk), :] = jnp.where(off + row <= qpos, _float_key(score), INT_MIN)
        return 0

    lax.fori_loop(0, nch, score_chunk, 0)

    def count(pred):
        def body(c, part):
            off = pl.multiple_of(c * tk, tk)
            hit = jnp.where(pred(key_ref[pl.ds(off, tk), :], off), 1.0, 0.0)
            return part + jnp.sum(hit.reshape(tk // SUBLANES, SUBLANES, tq), axis=0)
        part = lax.fori_loop(0, nch, body, jnp.zeros((SUBLANES, tq), F32))
        return jnp.sum(part, axis=0, keepdims=True)

    def search(it, t):
        cand = t + (jnp.int32(1) << (31 - it))
        cnt = count(lambda kc, off: kc >= cand)
        return jnp.where(cnt >= topk, cand, t)

    thr = lax.fori_loop(0, 32, search, jnp.full((1, tq), INT_MIN, I32))

    n_gt = count(lambda kc, off: kc > thr)
    n_eq = count(lambda kc, off: kc == thr)
    need = topk - n_gt
    tie = (thr > INT_MIN) & (n_eq > need)
    cut_ref[...] = jnp.full(cut_ref.shape, seq, I32)

    @pl.when(jnp.max(jnp.where(tie, 1.0, 0.0)) > 0.0)
    def _():
        nbits = max(1, (seq - 1).bit_length())

        def find(it, p):
            cand = p + (jnp.int32(1) << (nbits - 1 - it))
            cnt = count(lambda kc, off: (kc == thr) & (off + row < cand))
            return jnp.where(cnt < need, cand, p)

        cut = lax.fori_loop(0, nbits, find, jnp.zeros((1, tq), I32))
        cut_ref[...] = jnp.broadcast_to(jnp.where(tie, cut, seq), cut_ref.shape)

    _attend_init(m_ref, l_ref, acc_ref)
    cut1 = cut_ref[0:1, :]

    def attend(c, _):
        off = pl.multiple_of(c * tk, tk)
        key = key_ref[pl.ds(off, tk), :]
        kpos = off + row
        sel = ((key > thr) | ((key == thr) & (kpos <= cut1))) & (kpos <= qpos)
        bias = jnp.where(sel, 0.0, MASK_BIAS)
        kc = kk_ref[pl.ds(off, tk), :][:, IDX_DIM:]
        vt = vc_ref[0, :, pl.ds(off, tk)]
        _attend_heads(lambda hd: kc, lambda hd: qc_ref[0, hd * HD:(hd + 1) * HD, :], lambda hd: vt,
                      bias, m_ref, l_ref, acc_ref, DSA_HEADS, HD)
        return 0

    lax.fori_loop(0, nch, attend, 0)
    _attend_finish(o_ref, l_ref, acc_ref, DSA_HEADS, HD)


def _dsa(qi, kk, iw, qc, vc, tq, tk):
    B, QR, S = qc.shape
    nq = S // tq
    topk = min(TOPK_MAX, S // 4)
    return pl.pallas_call(
        functools.partial(_dsa_kernel, tq=tq, tk=tk, topk=topk, seq=S),
        grid=(B, nq),
        in_specs=[pl.BlockSpec((1, QR, tq), lambda b, i: (b, 0, i)),
                  pl.BlockSpec((S, LANES), lambda b, i: (b, 0)),
                  pl.BlockSpec((1, IDX_HEADS, tq), lambda b, i: (b, 0, i)),
                  pl.BlockSpec((1, QR, tq), lambda b, i: (b, 0, i)),
                  pl.BlockSpec((1, DSA_HEAD_DIM, S), lambda b, i: (b, 0, 0))],
        out_specs=pl.BlockSpec((tq, QR), lambda b, i: (b * nq + i, 0)),
        out_shape=jax.ShapeDtypeStruct((B * S, QR), BF16),
        scratch_shapes=[pltpu.VMEM((S, tq), I32),
                        pltpu.VMEM((SUBLANES, tq), I32),
                        pltpu.VMEM((DSA_HEADS, SUBLANES, tq), F32),
                        pltpu.VMEM((DSA_HEADS, SUBLANES, tq), F32),
                        pltpu.VMEM((QR, tq), F32)],
        compiler_params=_cparams(("arbitrary", "arbitrary")),
        name="dsa_attention",
    )(qi, kk, iw, qc, vc)


def _merge_kernel(x_ref, sh1_ref, sc1_ref, g1_ref, sh2_ref, sc2_ref, n1_ref, n2_ref,
                  oa_ref, ob_ref, oc_ref, wg_ref, bg_ref, wb_ref, wo_ref, rw_ref, rb_ref,
                  xo_ref, h2_ref, idx_ref, gate_ref):
    x = x_ref[...]
    tm, D = x.shape
    ms = jnp.mean(x * x, axis=-1, keepdims=True)
    h = (x * lax.rsqrt(ms + NORM_EPS) * n1_ref[...]) * (1.0 + sc1_ref[0]) + sh1_ref[0]
    hb = h.astype(BF16)
    mix = jnp.zeros(x.shape, F32)
    for n, o_ref in enumerate((oa_ref, ob_ref, oc_ref)):
        y = jnp.dot(o_ref[...], wb_ref[n], preferred_element_type=F32)
        gl = jnp.dot(hb, wg_ref[:, n * D:(n + 1) * D], preferred_element_type=F32) + bg_ref[:, n * D:(n + 1) * D]
        mix = mix + jax.nn.sigmoid(gl) * y
    out = jnp.dot(mix.astype(BF16), wo_ref[...], preferred_element_type=F32)
    xn = x + g1_ref[0] * out
    xo_ref[...] = xn
    ms2 = jnp.mean(xn * xn, axis=-1, keepdims=True)
    h2 = (xn * lax.rsqrt(ms2 + NORM_EPS) * n2_ref[...]) * (1.0 + sc2_ref[0]) + sh2_ref[0]
    for s in range(D // LANES):
        h2_ref[pl.ds(s, tm, stride=D // LANES), :] = h2[:, s * LANES:(s + 1) * LANES]
    logits = lax.dot_general(rw_ref[...], h2.astype(BF16), NT, preferred_element_type=F32) + rb_ref[...]
    eidx = lax.broadcasted_iota(I32, logits.shape, 0)
    vals, idxs = [], []
    for _ in range(TOP_K):
        mx = jnp.max(logits, axis=0, keepdims=True)
        am = jnp.min(jnp.where(logits == mx, eidx, N_EXPERTS), axis=0, keepdims=True)
        vals.append(mx)
        idxs.append(am)
        logits = jnp.where(eidx == am, -jnp.inf, logits)
    ex = [jnp.exp(v - vals[0]) for v in vals]
    den = ex[0] + ex[1] + ex[2] + ex[3]
    pad_i = [jnp.zeros_like(idxs[0])] * (SUBLANES - TOP_K)
    pad_f = [jnp.zeros_like(den)] * (SUBLANES - TOP_K)
    idx_ref[...] = jnp.concatenate(idxs + pad_i, axis=0)
    gate_ref[...] = jnp.concatenate([e / den for e in ex] + pad_f, axis=0)


def _merge(x2, mods, n1, n2, oa, ob, oc, wts, B, S, tm):
    T, D = x2.shape
    tps = S // tm
    R = D // LANES
    row = lambda i: (i, 0)
    bat = lambda i: (i // tps, 0, 0)
    full2 = lambda i: (0, 0)
    full3 = lambda i: (0, 0, 0)
    mod_spec = pl.BlockSpec((1, 1, D), bat)
    W = BRANCH_WIDTH
    return pl.pallas_call(
        _merge_kernel,
        grid=(T // tm,),
        in_specs=[pl.BlockSpec((tm, D), row)] + [mod_spec] * 5 + [
            pl.BlockSpec((1, D), full2), pl.BlockSpec((1, D), full2),
            pl.BlockSpec((tm, W), row), pl.BlockSpec((tm, W), row), pl.BlockSpec((tm, W), row),
            pl.BlockSpec((D, N_BRANCH * D), full2), pl.BlockSpec((1, N_BRANCH * D), full2),
            pl.BlockSpec((N_BRANCH, W, D), full3), pl.BlockSpec((D, D), full2),
            pl.BlockSpec((N_EXPERTS, D), full2), pl.BlockSpec((N_EXPERTS, 1), full2)],
        out_specs=[pl.BlockSpec((tm, D), row), pl.BlockSpec((tm * R, LANES), row),
                   pl.BlockSpec((SUBLANES, tm), lambda i: (0, i)), pl.BlockSpec((SUBLANES, tm), lambda i: (0, i))],
        out_shape=[jax.ShapeDtypeStruct((T, D), F32), jax.ShapeDtypeStruct((T * R, LANES), F32),
                   jax.ShapeDtypeStruct((SUBLANES, T), I32), jax.ShapeDtypeStruct((SUBLANES, T), F32)],
        compiler_params=_cparams(("arbitrary",)),
        name="merge_router",
    )(x2, *mods, n1, n2, oa, ob, oc, wts["w_gate"], wts["b_gate"], wts["w_branch"], wts["w_out"],
      wts["router_wt"], wts["router_b"])


def _deint_kernel(w_ref, p_ref, o_ref):
    o_ref[0] = jnp.dot(w_ref[0, 0].astype(BF16), p_ref[...], preferred_element_type=F32).astype(BF16)


def _deinterleave_w1(exp_w1, l):
    _, E, D, F2 = exp_w1.shape
    half = DEINT // 2
    j = jnp.arange(half)
    perm = jnp.zeros((DEINT, DEINT), BF16).at[2 * j, j].set(1.0).at[2 * j + 1, half + j].set(1.0)
    return pl.pallas_call(
        _deint_kernel,
        grid=(E, F2 // DEINT),
        in_specs=[pl.BlockSpec((1, 1, D, DEINT), lambda e, b: (l, e, 0, b)),
                  pl.BlockSpec((DEINT, DEINT), lambda e, b: (0, 0))],
        out_specs=pl.BlockSpec((1, D, DEINT), lambda e, b: (e, 0, b)),
        out_shape=jax.ShapeDtypeStruct((E, D, F2), BF16),
        compiler_params=_cparams(("arbitrary", "arbitrary")),
        name="w1_deinterleave",
    )(exp_w1, perm)


def _moe_kernel(be_ref, nu_ref, tok0_ref, tok1_ref, tokn_ref, dstp_ref, x_hbm, w1_ref, w2_ref, b1_ref, b2_ref,
                y_hbm, xbuf, ybuf, gsem, ssem, *, tb, hchunk, n_real):
    i = pl.program_id(0)
    nused = nu_ref[0]
    slot = i % 2
    other = 1 - slot
    gslot = i % 3
    R = SUBLANES

    def gather_copy(tok_ref, s, r):
        src = pl.multiple_of(tok_ref[0, 0, r], R)
        return pltpu.make_async_copy(x_hbm.at[pl.ds(src, R)], xbuf.at[s, pl.ds(r * R, R)], gsem.at[s])

    def scatter_copy(s, r):
        dst = pl.multiple_of(dstp_ref[0, 0, r], R)
        return pltpu.make_async_copy(ybuf.at[s, pl.ds(r * R, R)], y_hbm.at[pl.ds(dst, R)], ssem.at[s])

    def wait_gather(s):
        pltpu.make_async_copy(x_hbm.at[pl.ds(0, tb * R)], xbuf.at[s], gsem.at[s]).wait()

    def wait_scatter(s):
        pltpu.make_async_copy(ybuf.at[s], y_hbm.at[pl.ds(0, tb * R)], ssem.at[s]).wait()

    @pl.when(i == 0)
    def _():
        ybuf[...] = jnp.zeros(ybuf.shape, F32)
        for s in range(2):
            fill = pltpu.make_async_copy(ybuf.at[s], y_hbm.at[pl.ds((n_real + s * tb) * R, tb * R)], ssem.at[s])
            fill.start()
            fill.wait()

        def body(r, _):
            for s, tok_ref in enumerate((tok0_ref, tok1_ref)):
                src = pl.multiple_of(tok_ref[0, 0, r], R)
                pltpu.make_async_copy(x_hbm.at[pl.ds(src, R)], xbuf.at[s, pl.ds(pl.multiple_of(r * R, R), R)],
                                      gsem.at[s]).start()
            return 0
        lax.fori_loop(0, tb, body, 0, unroll=8)

    @pl.when(i < nused)
    def _():
        wait_gather(gslot)
        D = R * LANES
        xb = jnp.concatenate([xbuf[gslot, pl.ds(s, tb, stride=R), :] for s in range(R)], axis=1).astype(BF16)
        ahead = (i + 2) % 3
        for r in range(tb):
            scatter_copy(other, r).start(priority=r % 2)
            gather_copy(tokn_ref, ahead, r).start(priority=(r + 1) % 2)
        y = jnp.zeros((tb, D), F32) + b2_ref[0]
        q = DEINT // 2
        for j in range(D // hchunk):
            cs = slice(2 * j * hchunk, 2 * (j + 1) * hchunk)
            hgu = jnp.dot(xb, w1_ref[0, :, cs], preferred_element_type=F32) + b1_ref[0, :, cs]
            nb = 2 * hchunk // DEINT
            hg = jnp.concatenate([hgu[:, b * DEINT:b * DEINT + q] for b in range(nb)], axis=1)
            hu = jnp.concatenate([hgu[:, b * DEINT + q:(b + 1) * DEINT] for b in range(nb)], axis=1)
            gate = jnp.minimum(hg, SWIGLU_LIMIT)
            up = jnp.clip(hu, -SWIGLU_LIMIT, SWIGLU_LIMIT)
            act = (up + 1.0) * (gate * jax.nn.sigmoid(gate * SWIGLU_ALPHA))
            y = y + jnp.dot(act.astype(BF16), w2_ref[0, j * hchunk:(j + 1) * hchunk, :],
                            preferred_element_type=F32)
        @pl.when(i >= 1)
        def _():
            wait_scatter(slot)

        for s in range(R):
            ybuf[slot, pl.ds(s, tb, stride=R), :] = y[:, s * LANES:(s + 1) * LANES]

    @pl.when(i == nused)
    def _():
        wait_scatter(slot)
        wait_gather(gslot)
        wait_gather((i + 1) % 3)

        def body(r, _):
            dst = pl.multiple_of(dstp_ref[0, 0, r], R)
            pltpu.make_async_copy(ybuf.at[other, pl.ds(pl.multiple_of(r * R, R), R)], y_hbm.at[pl.ds(dst, R)],
                                  ssem.at[other]).start()
            return 0
        lax.fori_loop(0, tb, body, 0, unroll=8)
        wait_scatter(other)


def _moe(h2t, route, wts, tb, n_real):
    R = SUBLANES
    D = R * LANES
    block_expert, nused, tok_blk, dst_shift = route
    nblk = tok_blk.shape[0]
    tok3 = tok_blk.reshape(nblk, 1, tb)
    dst3 = dst_shift.reshape(nblk + 1, 1, tb)
    smem_blk = lambda f: pl.BlockSpec((1, 1, tb), f, memory_space=pltpu.SMEM)
    F2 = wts["w1"].shape[2]
    blk = lambda i, be: be[jnp.minimum(i, nblk - 1)]
    grid_spec = pltpu.PrefetchScalarGridSpec(
        num_scalar_prefetch=2,
        grid=(nblk + 1,),
        in_specs=[smem_blk(lambda i, be, nu: (0, 0, 0)),
                  smem_blk(lambda i, be, nu: (min(1, nblk - 1), 0, 0)),
                  smem_blk(lambda i, be, nu: (jnp.minimum(i + 2, nblk - 1), 0, 0)),
                  smem_blk(lambda i, be, nu: (i, 0, 0)),
                  pl.BlockSpec(memory_space=pl.ANY),
                  pl.BlockSpec((1, D, F2), lambda i, be, nu: (blk(i, be), 0, 0)),
                  pl.BlockSpec((1, F2 // 2, D), lambda i, be, nu: (blk(i, be), 0, 0)),
                  pl.BlockSpec((1, 1, F2), lambda i, be, nu: (blk(i, be), 0, 0)),
                  pl.BlockSpec((1, 1, D), lambda i, be, nu: (blk(i, be), 0, 0))],
        out_specs=pl.BlockSpec(memory_space=pl.ANY),
        scratch_shapes=[pltpu.VMEM((3, tb * R, LANES), F32), pltpu.VMEM((2, tb * R, LANES), F32),
                        pltpu.SemaphoreType.DMA((3,)), pltpu.SemaphoreType.DMA((2,))],
    )
    return pl.pallas_call(
        functools.partial(_moe_kernel, tb=tb, hchunk=256, n_real=n_real),
        grid_spec=grid_spec,
        out_shape=jax.ShapeDtypeStruct(((n_real + 2 * tb) * R, LANES), F32),
        compiler_params=_cparams(("arbitrary",)),
        name="moe_ffn",
    )(block_expert, nused, tok3, tok3, tok3, dst3, h2t, wts["w1"], wts["w2"], wts["b1"], wts["b2"])


def _route(top_idx, tb):
    E = N_EXPERTS
    R = SUBLANES
    T = top_idx.shape[0]
    N = T * TOP_K
    assert E * N < 2 ** 31
    nblk = N // tb + E
    keys = jnp.sort(top_idx.reshape(N) * N + jnp.arange(N, dtype=I32))
    order = keys % N
    starts = jnp.searchsorted(keys, jnp.arange(E + 1, dtype=I32) * N).astype(I32)
    counts = starts[1:] - starts[:-1]
    nb = (counts + tb - 1) // tb
    bends = jnp.cumsum(nb)
    nused = bends[E - 1]
    blk = jnp.arange(nblk, dtype=I32)
    be = jnp.minimum(jnp.searchsorted(bends, jnp.minimum(blk, nused - 1), side='right'), E - 1).astype(I32)
    j = blk - (bends - nb)[be]
    first = starts[be] + j * tb
    length = jnp.where(blk < nused, jnp.clip(counts[be] - j * tb, 0, tb), 0)
    r = jnp.arange(tb, dtype=I32)
    slot_blk = order[jnp.clip(first[:, None] + r[None, :], 0, N - 1)]
    tok_blk = (slot_blk // TOP_K) * R
    dump = N + (blk % 2)[:, None] * tb + r[None, :]
    dst_blk = jnp.where(r[None, :] < length[:, None], slot_blk, dump) * R
    dst_shift = jnp.concatenate([(N + tb + r[None, :]) * R, dst_blk], axis=0)
    return be, nused.reshape(1).astype(I32), tok_blk, dst_shift


def _combine_kernel(x_ref, g2_ref, gate_ref, y_ref, fg_ref, o_ref, *, final):
    g = gate_ref[...]
    tm, D = x_ref.shape
    R = D // LANES
    cols = []
    for s in range(R):
        cols.append(sum(g[:, k:k + 1] * y_ref[pl.ds(k * R + s, tm, stride=TOP_K * R), :] for k in range(TOP_K)))
    x = x_ref[...] + g2_ref[0] * jnp.concatenate(cols, axis=1)
    if final:
        x = x * lax.rsqrt(jnp.mean(x * x, axis=-1, keepdims=True) + NORM_EPS) * fg_ref[...]
    o_ref[...] = x


def _combine(x2, g2, gates, y4, final_g, B, S, tm, final):
    T, D = x2.shape
    R = D // LANES
    tps = S // tm
    row = lambda i: (i, 0)
    return pl.pallas_call(
        functools.partial(_combine_kernel, final=final),
        grid=(T // tm,),
        in_specs=[pl.BlockSpec((tm, D), row),
                  pl.BlockSpec((1, 1, D), lambda i: (i // tps, 0, 0)),
                  pl.BlockSpec((tm, TOP_K), row),
                  pl.BlockSpec((tm * TOP_K * R, LANES), row),
                  pl.BlockSpec((1, D), lambda i: (0, 0))],
        out_specs=pl.BlockSpec((tm, D), row),
        out_shape=jax.ShapeDtypeStruct((T, D), F32),
        compiler_params=_cparams(("arbitrary",)),
        name="combine",
    )(x2, g2, gates, y4, final_g)


def _pack_layer(l, w_in, b_gate, mla_q_norm, mla_w_uq, mla_kv_norm, mla_w_ukv, conv_w, w_branch, w_out,
                router_w, router_b, exp_w1, exp_b1, exp_w2, exp_b2):
    D = w_in.shape[1]
    wi = w_in[l]
    o = [0]

    def seg(n):
        s = wi[:, o[0]:o[0] + n]
        o[0] += n
        return s
    q_lat, kv_full, conv = seg(MLA_Q_LORA), seg(MLA_KV_LORA + MLA_ROPE), seg(3 * CONV_DIM)
    dq, dkv, iq = seg(DSA_HEADS * DSA_HEAD_DIM), seg(2 * DSA_HEAD_DIM), seg(IDX_HEADS * IDX_DIM)
    ik, iw, gates = seg(IDX_DIM), seg(IDX_HEADS), seg(N_BRANCH * D)
    zeros = lambda n: jnp.zeros((D, n), wi.dtype)
    w_main = jnp.concatenate([
        q_lat, kv_full[:, :MLA_KV_LORA],
        kv_full[:, MLA_KV_LORA:], zeros(LANES - MLA_ROPE),
        ik, dkv[:, :DSA_HEAD_DIM],
        conv], axis=1).astype(BF16)
    assert w_main.shape[1] == MAIN_COLS
    w_t = jnp.concatenate([dq, iq, dkv[:, DSA_HEAD_DIM:], iw, zeros(T_ROWS - ROW_IW - IDX_HEADS)],
                          axis=1).T.astype(BF16)
    assert w_t.shape[0] == T_ROWS
    H = MLA_HEADS
    hq = MLA_ROPE + MLA_NOPE
    w_uq = mla_w_uq[l].reshape(MLA_Q_LORA, H, hq)
    w_uq = jnp.pad(w_uq, ((0, 0), (0, 0), (0, HEAD_PAD - hq))).transpose(1, 2, 0).astype(BF16)
    w_ukv = mla_w_ukv[l].reshape(MLA_KV_LORA, H, MLA_NOPE + MLA_V)
    w_k = jnp.pad(w_ukv[:, :, :MLA_NOPE], ((0, 0), (0, 0), (MLA_ROPE, HEAD_PAD - hq)))
    w_k = w_k.transpose(1, 0, 2).astype(BF16)
    w_v = w_ukv[:, :, MLA_NOPE:].transpose(1, 2, 0).astype(BF16)
    b1 = exp_b1[l]
    E, F2 = b1.shape
    b1p = b1.reshape(E, F2 // DEINT, DEINT // 2, 2).transpose(0, 1, 3, 2).reshape(E, 1, F2)
    return dict(
        w_main=w_main, w_t=w_t,
        q_norm=mla_q_norm[l].reshape(1, -1), kv_norm=mla_kv_norm[l].reshape(1, -1),
        w_uq=w_uq, w_k=w_k, w_v=w_v, conv_w=conv_w[l],
        w_gate=gates.astype(BF16), b_gate=b_gate[l].reshape(1, -1),
        w_branch=w_branch[l].astype(BF16), w_out=w_out[l].astype(BF16),
        router_wt=router_w[l].T.astype(BF16), router_b=router_b[l].reshape(-1, 1),
        w1=_deinterleave_w1(exp_w1, l), w2=exp_w2[l].astype(BF16),
        b1=b1p, b2=exp_b2[l].reshape(E, 1, -1),
    )


def _tiles(S, T):
    tm = min(512, S)
    tq = min(256, S)
    tk = min(256, tq)
    N = T * TOP_K
    tb = 512 if N >= 512 * N_EXPERTS else 128
    return tm, tq, tk, tb


def kernel(x, c, positions, norm1_g, norm2_g, w_ada, b_ada, w_in, b_gate, mla_q_norm, mla_w_uq, mla_kv_norm,
           mla_w_ukv, conv_w, w_branch, w_out, router_w, router_b, exp_w1, exp_b1, exp_w2, exp_b2, final_g):
    B, S, D = x.shape
    T = B * S
    L = w_ada.shape[0]
    tm, tq, tk, tb = _tiles(S, T)
    mod = _ada(c, w_ada, b_ada)
    tabs = _rope_tables(positions)
    x2 = x.reshape(T, D)
    for l in range(L):
        wts = _pack_layer(l, w_in, b_gate, mla_q_norm, mla_w_uq, mla_kv_norm, mla_w_ukv, conv_w, w_branch,
                          w_out, router_w, router_b, exp_w1, exp_b1, exp_w2, exp_b2)
        sh1, sc1, g1, sh2, sc2, g2 = [m.reshape(B, 1, D) for m in jnp.split(mod[l], 6, axis=-1)]
        n1 = norm1_g[l].reshape(1, D)
        n2 = norm2_g[l].reshape(1, D)
        q, k, v, ob, qc, qi, kk, vc, iw = _front(x2, sh1, sc1, n1, wts, tabs, B, S, tm)
        oa = _mla(q, k, v, tq, tk)
        oc = _dsa(qi, kk, iw, qc, vc, tq, tk)
        x2, h2t, idx8, gate8 = _merge(x2, (sh1, sc1, g1, sh2, sc2), n1, n2, oa, ob, oc, wts, B, S, tm)
        y4 = _moe(h2t, _route(idx8[:TOP_K].T, tb), wts, tb, T * TOP_K)
        x2 = _combine(x2, g2, gate8[:TOP_K].T, y4, final_g.reshape(1, D), B, S, tm, final=(l == L - 1))
    return x2.reshape(B, S, D)
```

```python
import functools
import math

import jax
import jax.numpy as jnp
from jax import lax
from jax.experimental import pallas as pl
from jax.experimental.pallas import tpu as pltpu

F32 = jnp.float32
BF16 = jnp.bfloat16
I32 = jnp.int32

ROPE_THETA = 500000.0
NORM_EPS = 1e-6

MLA_HEADS = 8
MLA_NOPE = 64
MLA_ROPE = 32
MLA_V = 64
MLA_Q_LORA = 256
MLA_KV_LORA = 128
CONV_DIM = 512
CONV_WIDTH = 3
DSA_HEADS = 8
DSA_HEAD_DIM = 64
DSA_ROT = 16
IDX_HEADS = 8
IDX_DIM = 64
IDX_ROT = 16
TOPK_MAX = 256
N_BRANCH = 3
BRANCH_WIDTH = 512
N_EXPERTS = 32
TOP_K = 4
SWIGLU_ALPHA = 1.702
SWIGLU_LIMIT = 7.0

LANES = 128
SUBLANES = 8
HEAD_PAD = 128
INT_MIN = -(2 ** 31)
MASK_BIAS = -1e30
LOG2E = math.log2(math.e)
VMEM_LIMIT = 56 * 1024 * 1024
NT = (((1,), (1,)), ((), ()))

SEG_QLAT = 0
SEG_KVLAT = 256
SEG_KPE = 384
SEG_KK = 512
SEG_CONV = 640
MAIN_COLS = 2176
ROW_DQ = 0
ROW_IQ = 512
ROW_VC = 1024
ROW_IW = 1088
T_ROWS = 1104
DEINT = 256


def _cparams(sem):
    return pltpu.CompilerParams(dimension_semantics=sem, vmem_limit_bytes=VMEM_LIMIT)


def _rope(x, cos, sin, half, period):
    lane = lax.broadcasted_iota(I32, x.shape, 1)
    first = (lane % period) < half
    rot = jnp.where(first, pltpu.roll(x, LANES - half, axis=1), pltpu.roll(x, half, axis=1))
    return x * cos + rot * sin


def _rope_t(x, cos, sin, half, period):
    R = x.shape[0]
    row = lax.broadcasted_iota(I32, x.shape, 0)
    first = (row % period) < half
    rot = jnp.where(first, pltpu.roll(x, R - half, axis=0), pltpu.roll(x, half, axis=0))
    reps = R // period
    if reps > 1:
        cos = jnp.tile(cos, (reps, 1))
        sin = jnp.tile(sin, (reps, 1))
    return x * cos + rot * sin


def _ada_kernel(c_ref, w_ref, b_ref, o_ref):
    c = c_ref[...]
    act = c * jax.nn.sigmoid(c)
    o_ref[0] = jnp.dot(act, w_ref[0], preferred_element_type=F32,
                       precision=lax.Precision.HIGHEST) + b_ref[0]


def _ada(c, w_ada, b_ada):
    L, D, D6 = w_ada.shape
    B = c.shape[0]
    nblk = D6 // D
    return pl.pallas_call(
        _ada_kernel,
        grid=(L, nblk),
        in_specs=[pl.BlockSpec((B, D), lambda l, j: (0, 0)),
                  pl.BlockSpec((1, D, D), lambda l, j: (l, 0, j)),
                  pl.BlockSpec((1, 1, D), lambda l, j: (l, 0, j))],
        out_specs=pl.BlockSpec((1, B, D), lambda l, j: (l, 0, j)),
        out_shape=jax.ShapeDtypeStruct((L, B, D6), F32),
        compiler_params=_cparams(("arbitrary", "arbitrary")),
        name="adaln",
    )(c, w_ada, b_ada.reshape(L, 1, D6))


def _trig_kernel(freq_ref, pos_ref, cos_ref, sin_ref):
    ang = pos_ref[...].astype(F32) * freq_ref[pl.program_id(0)]
    cos_ref[0] = jnp.cos(ang)
    sin_ref[0] = jnp.sin(ang)


def _rope_tables(positions):
    B, S = positions.shape
    T = B * S
    ha, hc = MLA_ROPE // 2, DSA_ROT // 2
    fa = jnp.exp(-math.log(ROPE_THETA) * jnp.arange(ha, dtype=F32) * (2.0 / MLA_ROPE))
    fc = jnp.exp(-math.log(ROPE_THETA) * jnp.arange(hc, dtype=F32) * (2.0 / DSA_ROT))
    freqs = jnp.concatenate([fa, fc])
    nf = ha + hc
    cos, sin = pl.pallas_call(
        _trig_kernel,
        grid=(nf,),
        in_specs=[pl.BlockSpec(memory_space=pltpu.SMEM),
                  pl.BlockSpec((B, S), lambda i: (0, 0))],
        out_specs=[pl.BlockSpec((1, B, S), lambda i: (i, 0, 0)),
                   pl.BlockSpec((1, B, S), lambda i: (i, 0, 0))],
        out_shape=[jax.ShapeDtypeStruct((nf, B, S), F32)] * 2,
        compiler_params=_cparams(("arbitrary",)),
        name="rope_trig",
    )(freqs, positions)
    cos = cos.reshape(nf, T)
    sin = sin.reshape(nf, T)
    ca, sa, cc, sc = cos[:ha], sin[:ha], cos[ha:], sin[ha:]
    cos_at = jnp.concatenate([ca, ca, jnp.ones((HEAD_PAD - 2 * ha, T), F32)], axis=0)
    sin_at = jnp.concatenate([-sa, sa, jnp.zeros((HEAD_PAD - 2 * ha, T), F32)], axis=0)
    cos_ct = jnp.concatenate([cc, cc, jnp.ones((DSA_HEAD_DIM - 2 * hc, T), F32)], axis=0)
    sin_ct = jnp.concatenate([-sc, sc, jnp.zeros((DSA_HEAD_DIM - 2 * hc, T), F32)], axis=0)
    cos_a, sin_a = cos_at.T, sin_at.T
    cos_c = jnp.concatenate([cos_ct, cos_ct], axis=0).T
    sin_c = jnp.concatenate([sin_ct, sin_ct], axis=0).T
    return dict(cos_a=cos_a, sin_a=sin_a, cos_c=cos_c, sin_c=sin_c,
                cos_at=cos_at, sin_at=sin_at, cos_ct=cos_ct, sin_ct=sin_ct)


def _front_kernel(x_ref, sh_ref, sc_ref, g_ref, w_ref, wt_ref, qn_ref, wuq_ref, kvn_ref, wk_ref, wv_ref, cw_ref,
                  ca_ref, sa_ref, cc_ref, sn_ref, cat_ref, sat_ref, cct_ref, snt_ref,
                  q_out, k_out, v_out, ob_out, qc_out, qi_out, kk_out, vc_out, iw_out,
                  carry_ref, *, tiles_per_seq):
    i = pl.program_id(0)
    x = x_ref[...]
    ms = jnp.mean(x * x, axis=-1, keepdims=True)
    h = (x * lax.rsqrt(ms + NORM_EPS) * g_ref[...]) * (1.0 + sc_ref[0]) + sh_ref[0]
    hb = h.astype(BF16)

    def proj(start, width):
        return jnp.dot(hb, w_ref[:, start:start + width], preferred_element_type=F32)

    def proj_t(start, rows):
        return lax.dot_general(wt_ref[start:start + rows, :], hb, NT, preferred_element_type=F32)

    q_lat = proj(SEG_QLAT, MLA_Q_LORA)
    cq = q_lat * lax.rsqrt(jnp.mean(q_lat * q_lat, axis=-1, keepdims=True) + NORM_EPS) * qn_ref[...]
    cqb = cq.astype(BF16)
    kv_lat = proj(SEG_KVLAT, MLA_KV_LORA)
    ckv = kv_lat * lax.rsqrt(jnp.mean(kv_lat * kv_lat, axis=-1, keepdims=True) + NORM_EPS) * kvn_ref[...]
    ckvb = ckv.astype(BF16)
    k_pe = _rope(proj(SEG_KPE, LANES), ca_ref[...], sa_ref[...], MLA_ROPE // 2, LANES)
    cat, sat = cat_ref[...], sat_ref[...]
    q_scale = (MLA_NOPE + MLA_ROPE) ** -0.5 * LOG2E
    for hd in range(MLA_HEADS):
        qh = lax.dot_general(wuq_ref[hd], cqb, NT, preferred_element_type=F32)
        q_out[0, hd] = (_rope_t(qh, cat, sat, MLA_ROPE // 2, HEAD_PAD) * q_scale).astype(BF16)
        kh = jnp.dot(ckvb, wk_ref[hd], preferred_element_type=F32) + k_pe
        k_out[0, hd] = kh.astype(BF16)
        v_out[0, hd] = lax.dot_general(wv_ref[hd], ckvb, NT, preferred_element_type=F32).astype(BF16)

    g_b = proj(SEG_CONV, CONV_DIM)
    u = proj(SEG_CONV + CONV_DIM, CONV_DIM) * proj(SEG_CONV + 2 * CONV_DIM, CONV_DIM)
    tm = u.shape[0]

    @pl.when(i % tiles_per_seq == 0)
    def _():
        carry_ref[...] = jnp.zeros(carry_ref.shape, F32)

    prev = carry_ref[...]
    row = lax.broadcasted_iota(I32, u.shape, 0)
    u1 = jnp.where(row == 0, prev[7:8], pltpu.roll(u, 1, axis=0))
    u2 = jnp.where(row == 0, prev[6:7], jnp.where(row == 1, prev[7:8], pltpu.roll(u, 2, axis=0)))
    carry_ref[...] = u[tm - SUBLANES:tm]
    cw = cw_ref[...]
    y = cw[0:1] * u2 + cw[1:2] * u1 + cw[2:3] * u
    ob_out[...] = (g_b * y).astype(BF16)

    hc = DSA_ROT // 2
    cct, snt = cct_ref[...], snt_ref[...]
    nq_rows = DSA_HEADS * DSA_HEAD_DIM
    qc_scale = DSA_HEAD_DIM ** -0.5 * LOG2E
    qc_out[0] = (_rope_t(proj_t(ROW_DQ, nq_rows), cct, snt, hc, DSA_HEAD_DIM) * qc_scale).astype(BF16)
    qi_out[0] = (_rope_t(proj_t(ROW_IQ, nq_rows), cct, snt, hc, IDX_DIM) * IDX_DIM ** -0.5).astype(BF16)
    kk_out[...] = _rope(proj(SEG_KK, LANES), cc_ref[...], sn_ref[...], hc, DSA_HEAD_DIM).astype(BF16)
    vc_out[0] = proj_t(ROW_VC, DSA_HEAD_DIM).astype(BF16)
    iw_out[0] = proj_t(ROW_IW, IDX_HEADS) * IDX_HEADS ** -0.5


def _front(x2, shift, scale, norm_g, wts, tabs, B, S, tm):
    T, D = x2.shape
    tps = S // tm
    nt = T // tm
    row = lambda i: (i, 0)
    col = lambda i: (0, i)
    bat = lambda i: (i // tps, 0, 0)
    head = lambda i: (i // tps, 0, i % tps, 0)
    head_t = lambda i: (i // tps, 0, 0, i % tps)
    seq_t = lambda i: (i // tps, 0, i % tps)
    full2 = lambda i: (0, 0)
    full3 = lambda i: (0, 0, 0)
    H = MLA_HEADS
    QR = DSA_HEADS * DSA_HEAD_DIM
    in_specs = [
        pl.BlockSpec((tm, D), row),
        pl.BlockSpec((1, 1, D), bat),
        pl.BlockSpec((1, 1, D), bat),
        pl.BlockSpec((1, D), full2),
        pl.BlockSpec((D, MAIN_COLS), full2),
        pl.BlockSpec((T_ROWS, D), full2),
        pl.BlockSpec((1, MLA_Q_LORA), full2),
        pl.BlockSpec((H, HEAD_PAD, MLA_Q_LORA), full3),
        pl.BlockSpec((1, MLA_KV_LORA), full2),
        pl.BlockSpec((H, MLA_KV_LORA, HEAD_PAD), full3),
        pl.BlockSpec((H, MLA_V, MLA_KV_LORA), full3),
        pl.BlockSpec((CONV_WIDTH, CONV_DIM), full2),
        pl.BlockSpec((tm, LANES), row),
        pl.BlockSpec((tm, LANES), row),
        pl.BlockSpec((tm, LANES), row),
        pl.BlockSpec((tm, LANES), row),
        pl.BlockSpec((HEAD_PAD, tm), col),
        pl.BlockSpec((HEAD_PAD, tm), col),
        pl.BlockSpec((DSA_HEAD_DIM, tm), col),
        pl.BlockSpec((DSA_HEAD_DIM, tm), col),
    ]
    out_specs = [
        pl.BlockSpec((1, H, HEAD_PAD, tm), head_t),
        pl.BlockSpec((1, H, tm, HEAD_PAD), head),
        pl.BlockSpec((1, H, MLA_V, tm), head_t),
        pl.BlockSpec((tm, CONV_DIM), row),
        pl.BlockSpec((1, QR, tm), seq_t),
        pl.BlockSpec((1, QR, tm), seq_t),
        pl.BlockSpec((tm, LANES), row),
        pl.BlockSpec((1, DSA_HEAD_DIM, tm), seq_t),
        pl.BlockSpec((1, IDX_HEADS, tm), seq_t),
    ]
    out_shape = [
        jax.ShapeDtypeStruct((B, H, HEAD_PAD, S), BF16),
        jax.ShapeDtypeStruct((B, H, S, HEAD_PAD), BF16),
        jax.ShapeDtypeStruct((B, H, MLA_V, S), BF16),
        jax.ShapeDtypeStruct((T, CONV_DIM), BF16),
        jax.ShapeDtypeStruct((B, QR, S), BF16),
        jax.ShapeDtypeStruct((B, QR, S), BF16),
        jax.ShapeDtypeStruct((T, LANES), BF16),
        jax.ShapeDtypeStruct((B, DSA_HEAD_DIM, S), BF16),
        jax.ShapeDtypeStruct((B, IDX_HEADS, S), F32),
    ]
    return pl.pallas_call(
        functools.partial(_front_kernel, tiles_per_seq=tps),
        grid=(nt,),
        in_specs=in_specs,
        out_specs=out_specs,
        out_shape=out_shape,
        scratch_shapes=[pltpu.VMEM((SUBLANES, CONV_DIM), F32)],
        compiler_params=_cparams(("arbitrary",)),
        name="front",
    )(x2, shift, scale, norm_g, wts["w_main"], wts["w_t"], wts["q_norm"], wts["w_uq"], wts["kv_norm"],
      wts["w_k"], wts["w_v"], wts["conv_w"],
      tabs["cos_a"], tabs["sin_a"], tabs["cos_c"], tabs["sin_c"],
      tabs["cos_at"], tabs["sin_at"], tabs["cos_ct"], tabs["sin_ct"])


def _attend_heads(key_tile, q_tile, v_tile, bias, m_ref, l_ref, acc_ref, nheads, vdim):
    scores = [jnp.dot(key_tile(hd), q_tile(hd), preferred_element_type=F32) for hd in range(nheads)]
    probs, alphas = [], []
    for hd in range(nheads):
        s = scores[hd] if bias is None else scores[hd] + bias
        m_new = jnp.maximum(m_ref[hd], jnp.max(s, axis=0, keepdims=True))
        alpha = jnp.exp2(m_ref[hd] - m_new)
        p = jnp.exp2(s - m_new[0:1])
        l_ref[hd] = alpha * l_ref[hd] + jnp.sum(p, axis=0, keepdims=True)
        m_ref[hd] = m_new
        probs.append(p.astype(BF16))
        alphas.append(alpha[0:1])
    for hd in range(nheads):
        rows = slice(hd * vdim, (hd + 1) * vdim)
        acc_ref[rows, :] = alphas[hd] * acc_ref[rows, :] + jnp.dot(v_tile(hd), probs[hd],
                                                                   preferred_element_type=F32)


def _attend_init(m_ref, l_ref, acc_ref):
    m_ref[...] = jnp.full(m_ref.shape, MASK_BIAS, F32)
    l_ref[...] = jnp.zeros(l_ref.shape, F32)
    acc_ref[...] = jnp.zeros(acc_ref.shape, F32)


def _attend_finish(o_ref, l_ref, acc_ref, nheads, vdim):
    outs = [acc_ref[hd * vdim:(hd + 1) * vdim, :] / l_ref[hd, 0:1, :] for hd in range(nheads)]
    o_ref[...] = jnp.concatenate(outs, axis=0).T.astype(BF16)


def _mla_kernel(q_ref, k_ref, v_ref, o_ref, m_ref, l_ref, acc_ref, *, tq, tk):
    i = pl.program_id(1)
    sub = tq // tk
    _attend_init(m_ref, l_ref, acc_ref)

    def chunk(c, bias):
        off = pl.multiple_of(c * tk, tk)
        _attend_heads(lambda hd: k_ref[0, hd, pl.ds(off, tk), :],
                      lambda hd: q_ref[0, hd],
                      lambda hd: v_ref[0, hd, :, pl.ds(off, tk)],
                      bias, m_ref, l_ref, acc_ref, MLA_HEADS, MLA_V)

    def body(c, _):
        chunk(c, None)
        return 0

    lax.fori_loop(0, i * sub, body, 0)
    kpos = lax.broadcasted_iota(I32, (tk, tq), 0)
    qpos = lax.broadcasted_iota(I32, (tk, tq), 1)
    for j in range(sub):
        chunk(i * sub + j, jnp.where(j * tk + kpos <= qpos, 0.0, MASK_BIAS))
    _attend_finish(o_ref, l_ref, acc_ref, MLA_HEADS, MLA_V)


def _mla(q, k, v, tq, tk):
    B, H, S, _ = k.shape
    nq = S // tq
    return pl.pallas_call(
        functools.partial(_mla_kernel, tq=tq, tk=tk),
        grid=(B, nq),
        in_specs=[pl.BlockSpec((1, H, HEAD_PAD, tq), lambda b, i: (b, 0, 0, i)),
                  pl.BlockSpec((1, H, S, HEAD_PAD), lambda b, i: (b, 0, 0, 0)),
                  pl.BlockSpec((1, H, MLA_V, S), lambda b, i: (b, 0, 0, 0))],
        out_specs=pl.BlockSpec((tq, H * MLA_V), lambda b, i: (b * nq + i, 0)),
        out_shape=jax.ShapeDtypeStruct((B * S, H * MLA_V), BF16),
        scratch_shapes=[pltpu.VMEM((H, SUBLANES, tq), F32), pltpu.VMEM((H, SUBLANES, tq), F32),
                        pltpu.VMEM((H * MLA_V, tq), F32)],
        compiler_params=_cparams(("arbitrary", "arbitrary")),
        name="mla_attention",
    )(q, k, v)


def _float_key(score):
    bits = pltpu.bitcast(score + 0.0, I32)
    return jnp.where(bits >= 0, bits, bits ^ jnp.int32(0x7FFFFFFF))


def _dsa_kernel(qi_ref, kk_ref, iw_ref, qc_ref, vc_ref, o_ref,
                key_ref, cut_ref, m_ref, l_ref, acc_ref, *, tq, tk, topk, seq):
    i = pl.program_id(1)
    nch = (i + 1) * (tq // tk)
    q0 = i * tq
    row = lax.broadcasted_iota(I32, (tk, tq), 0)
    qpos = q0 + lax.broadcasted_iota(I32, (tk, tq), 1)
    HD = DSA_HEAD_DIM

    iw = iw_ref[0]

    def score_chunk(c, _):
        off = pl.multiple_of(c * tk, tk)
        kic = kk_ref[pl.ds(off, tk), :][:, :IDX_DIM]
        score = jnp.zeros((tk, tq), F32)
        for hd in range(IDX_HEADS):
            lg = jnp.dot(kic, qi_ref[0, hd * IDX_DIM:(hd + 1) * IDX_DIM, :], preferred_element_type=F32)
            score = score + jnp.maximum(lg, 0.0) * iw[hd:hd + 1, :]
        key_ref[pl.ds(off, tk), :] = jnp.where(off + row <= qpos, _float_key(score), INT_MIN)
        return 0

    lax.fori_loop(0, nch, score_chunk, 0)

    def count(pred):
        def body(c, part):
            off = pl.multiple_of(c * tk, tk)
            hit = jnp.where(pred(key_ref[pl.ds(off, tk), :], off), 1.0, 0.0)
            return part + jnp.sum(hit.reshape(tk // SUBLANES, SUBLANES, tq), axis=0)
        part = lax.fori_loop(0, nch, body, jnp.zeros((SUBLANES, tq), F32))
        return jnp.sum(part, axis=0, keepdims=True)

    def search(it, t):
        cand = t + (jnp.int32(1) << (31 - it))
        cnt = count(lambda kc, off: kc >= cand)
        return jnp.where(cnt >= topk, cand, t)

    thr = lax.fori_loop(0, 32, search, jnp.full((1, tq), INT_MIN, I32))

    n_gt = count(lambda kc, off: kc > thr)
    n_eq = count(lambda kc, off: kc == thr)
    need = topk - n_gt
    tie = (thr > INT_MIN) & (n_eq > need)
    cut_ref[...] = jnp.full(cut_ref.shape, seq, I32)

    @pl.when(jnp.max(jnp.where(tie, 1.0, 0.0)) > 0.0)
    def _():
        nbits = max(1, (seq - 1).bit_length())

        def find(it, p):
            cand = p + (jnp.int32(1) << (nbits - 1 - it))
            cnt = count(lambda kc, off: (kc == thr) & (off + row < cand))
            return jnp.where(cnt < need, cand, p)

        cut = lax.fori_loop(0, nbits, find, jnp.zeros((1, tq), I32))
        cut_ref[...] = jnp.broadcast_to(jnp.where(tie, cut, seq), cut_ref.shape)

    _attend_init(m_ref, l_ref, acc_ref)
    cut1 = cut_ref[0:1, :]

    def attend(c, _):
        off = pl.multiple_of(c * tk, tk)
        key = key_ref[pl.ds(off, tk), :]
        kpos = off + row
        sel = ((key > thr) | ((key == thr) & (kpos <= cut1))) & (kpos <= qpos)
        bias = jnp.where(sel, 0.0, MASK_BIAS)
        kc = kk_ref[pl.ds(off, tk), :][:, IDX_DIM:]
        vt = vc_ref[0, :, pl.ds(off, tk)]
        _attend_heads(lambda hd: kc, lambda hd: qc_ref[0, hd * HD:(hd + 1) * HD, :], lambda hd: vt,
                      bias, m_ref, l_ref, acc_ref, DSA_HEADS, HD)
        return 0

    lax.fori_loop(0, nch, attend, 0)
    _attend_finish(o_ref, l_ref, acc_ref, DSA_HEADS, HD)


def _dsa(qi, kk, iw, qc, vc, tq, tk):
    B, QR, S = qc.shape
    nq = S // tq
    topk = min(TOPK_MAX, S // 4)
    return pl.pallas_call(
        functools.partial(_dsa_kernel, tq=tq, tk=tk, topk=topk, seq=S),
        grid=(B, nq),
        in_specs=[pl.BlockSpec((1, QR, tq), lambda b, i: (b, 0, i)),
                  pl.BlockSpec((S, LANES), lambda b, i: (b, 0)),
                  pl.BlockSpec((1, IDX_HEADS, tq), lambda b, i: (b, 0, i)),
                  pl.BlockSpec((1, QR, tq), lambda b, i: (b, 0, i)),
                  pl.BlockSpec((1, DSA_HEAD_DIM, S), lambda b, i: (b, 0, 0))],
        out_specs=pl.BlockSpec((tq, QR), lambda b, i: (b * nq + i, 0)),
        out_shape=jax.ShapeDtypeStruct((B * S, QR), BF16),
        scratch_shapes=[pltpu.VMEM((S, tq), I32),
                        pltpu.VMEM((SUBLANES, tq), I32),
                        pltpu.VMEM((DSA_HEADS, SUBLANES, tq), F32),
                        pltpu.VMEM((DSA_HEADS, SUBLANES, tq), F32),
                        pltpu.VMEM((QR, tq), F32)],
        compiler_params=_cparams(("arbitrary", "arbitrary")),
        name="dsa_attention",
    )(qi, kk, iw, qc, vc)


def _merge_kernel(x_ref, sh1_ref, sc1_ref, g1_ref, sh2_ref, sc2_ref, n1_ref, n2_ref,
                  oa_ref, ob_ref, oc_ref, wg_ref, bg_ref, wb_ref, wo_ref, rw_ref, rb_ref,
                  xo_ref, h2_ref, idx_ref, gate_ref):
    x = x_ref[...]
    tm, D = x.shape
    ms = jnp.mean(x * x, axis=-1, keepdims=True)
    h = (x * lax.rsqrt(ms + NORM_EPS) * n1_ref[...]) * (1.0 + sc1_ref[0]) + sh1_ref[0]
    hb = h.astype(BF16)
    mix = jnp.zeros(x.shape, F32)
    for n, o_ref in enumerate((oa_ref, ob_ref, oc_ref)):
        y = jnp.dot(o_ref[...], wb_ref[n], preferred_element_type=F32)
        gl = jnp.dot(hb, wg_ref[:, n * D:(n + 1) * D], preferred_element_type=F32) + bg_ref[:, n * D:(n + 1) * D]
        mix = mix + jax.nn.sigmoid(gl) * y
    out = jnp.dot(mix.astype(BF16), wo_ref[...], preferred_element_type=F32)
    xn = x + g1_ref[0] * out
    xo_ref[...] = xn
    ms2 = jnp.mean(xn * xn, axis=-1, keepdims=True)
    h2 = (xn * lax.rsqrt(ms2 + NORM_EPS) * n2_ref[...]) * (1.0 + sc2_ref[0]) + sh2_ref[0]
    for s in range(D // LANES):
        h2_ref[pl.ds(s, tm, stride=D // LANES), :] = h2[:, s * LANES:(s + 1) * LANES]
    logits = lax.dot_general(rw_ref[...], h2.astype(BF16), NT, preferred_element_type=F32) + rb_ref[...]
    eidx = lax.broadcasted_iota(I32, logits.shape, 0)
    vals, idxs = [], []
    for _ in range(TOP_K):
        mx = jnp.max(logits, axis=0, keepdims=True)
        am = jnp.min(jnp.where(logits == mx, eidx, N_EXPERTS), axis=0, keepdims=True)
        vals.append(mx)
        idxs.append(am)
        logits = jnp.where(eidx == am, -jnp.inf, logits)
    ex = [jnp.exp(v - vals[0]) for v in vals]
    den = ex[0] + ex[1] + ex[2] + ex[3]
    pad_i = [jnp.zeros_like(idxs[0])] * (SUBLANES - TOP_K)
    pad_f = [jnp.zeros_like(den)] * (SUBLANES - TOP_K)
    idx_ref[...] = jnp.concatenate(idxs + pad_i, axis=0)
    gate_ref[...] = jnp.concatenate([e / den for e in ex] + pad_f, axis=0)


def _merge(x2, mods, n1, n2, oa, ob, oc, wts, B, S, tm):
    T, D = x2.shape
    tps = S // tm
    R = D // LANES
    row = lambda i: (i, 0)
    bat = lambda i: (i // tps, 0, 0)
    full2 = lambda i: (0, 0)
    full3 = lambda i: (0, 0, 0)
    mod_spec = pl.BlockSpec((1, 1, D), bat)
    W = BRANCH_WIDTH
    return pl.pallas_call(
        _merge_kernel,
        grid=(T // tm,),
        in_specs=[pl.BlockSpec((tm, D), row)] + [mod_spec] * 5 + [
            pl.BlockSpec((1, D), full2), pl.BlockSpec((1, D), full2),
            pl.BlockSpec((tm, W), row), pl.BlockSpec((tm, W), row), pl.BlockSpec((tm, W), row),
            pl.BlockSpec((D, N_BRANCH * D), full2), pl.BlockSpec((1, N_BRANCH * D), full2),
            pl.BlockSpec((N_BRANCH, W, D), full3), pl.BlockSpec((D, D), full2),
            pl.BlockSpec((N_EXPERTS, D), full2), pl.BlockSpec((N_EXPERTS, 1), full2)],
        out_specs=[pl.BlockSpec((tm, D), row), pl.BlockSpec((tm * R, LANES), row),
                   pl.BlockSpec((SUBLANES, tm), lambda i: (0, i)), pl.BlockSpec((SUBLANES, tm), lambda i: (0, i))],
        out_shape=[jax.ShapeDtypeStruct((T, D), F32), jax.ShapeDtypeStruct((T * R, LANES), F32),
                   jax.ShapeDtypeStruct((SUBLANES, T), I32), jax.ShapeDtypeStruct((SUBLANES, T), F32)],
        compiler_params=_cparams(("arbitrary",)),
        name="merge_router",
    )(x2, *mods, n1, n2, oa, ob, oc, wts["w_gate"], wts["b_gate"], wts["w_branch"], wts["w_out"],
      wts["router_wt"], wts["router_b"])


def _deint_kernel(w_ref, p_ref, o_ref):
    o_ref[0] = jnp.dot(w_ref[0, 0].astype(BF16), p_ref[...], preferred_element_type=F32).astype(BF16)


def _deinterleave_w1(exp_w1, l):
    _, E, D, F2 = exp_w1.shape
    half = DEINT // 2
    j = jnp.arange(half)
    perm = jnp.zeros((DEINT, DEINT), BF16).at[2 * j, j].set(1.0).at[2 * j + 1, half + j].set(1.0)
    return pl.pallas_call(
        _deint_kernel,
        grid=(E, F2 // DEINT),
        in_specs=[pl.BlockSpec((1, 1, D, DEINT), lambda e, b: (l, e, 0, b)),
                  pl.BlockSpec((DEINT, DEINT), lambda e, b: (0, 0))],
        out_specs=pl.BlockSpec((1, D, DEINT), lambda e, b: (e, 0, b)),
        out_shape=jax.ShapeDtypeStruct((E, D, F2), BF16),
        compiler_params=_cparams(("arbitrary", "arbitrary")),
        name="w1_deinterleave",
    )(exp_w1, perm)


def _moe_kernel(be_ref, nu_ref, tok0_ref, tok1_ref, tokn_ref, dstp_ref, x_hbm, w1_ref, w2_ref, b1_ref, b2_ref,
                y_hbm, xbuf, ybuf, gsem, ssem, *, tb, hchunk, n_real):
    i = pl.program_id(0)
    nused = nu_ref[0]
    slot = i % 2
    other = 1 - slot
    gslot = i % 3
    R = SUBLANES

    def gather_copy(tok_ref, s, r):
        src = pl.multiple_of(tok_ref[0, 0, r], R)
        return pltpu.make_async_copy(x_hbm.at[pl.ds(src, R)], xbuf.at[s, pl.ds(r * R, R)], gsem.at[s])

    def scatter_copy(s, r):
        dst = pl.multiple_of(dstp_ref[0, 0, r], R)
        return pltpu.make_async_copy(ybuf.at[s, pl.ds(r * R, R)], y_hbm.at[pl.ds(dst, R)], ssem.at[s])

    def wait_gather(s):
        pltpu.make_async_copy(x_hbm.at[pl.ds(0, tb * R)], xbuf.at[s], gsem.at[s]).wait()

    def wait_scatter(s):
        pltpu.make_async_copy(ybuf.at[s], y_hbm.at[pl.ds(0, tb * R)], ssem.at[s]).wait()

    @pl.when(i == 0)
    def _():
        ybuf[...] = jnp.zeros(ybuf.shape, F32)
        for s in range(2):
            fill = pltpu.make_async_copy(ybuf.at[s], y_hbm.at[pl.ds((n_real + s * tb) * R, tb * R)], ssem.at[s])
            fill.start()
            fill.wait()

        def body(r, _):
            for s, tok_ref in enumerate((tok0_ref, tok1_ref)):
                src = pl.multiple_of(tok_ref[0, 0, r], R)
                pltpu.make_async_copy(x_hbm.at[pl.ds(src, R)], xbuf.at[s, pl.ds(pl.multiple_of(r * R, R), R)],
                                      gsem.at[s]).start()
            return 0
        lax.fori_loop(0, tb, body, 0, unroll=8)

    @pl.when(i < nused)
    def _():
        wait_gather(gslot)
        D = R * LANES
        xb = jnp.concatenate([xbuf[gslot, pl.ds(s, tb, stride=R), :] for s in range(R)], axis=1).astype(BF16)
        ahead = (i + 2) % 3
        for r in range(tb):
            scatter_copy(other, r).start(priority=r % 2)
            gather_copy(tokn_ref, ahead, r).start(priority=(r + 1) % 2)
        y = jnp.zeros((tb, D), F32) + b2_ref[0]
        q = DEINT // 2
        for j in range(D // hchunk):
            cs = slice(2 * j * hchunk, 2 * (j + 1) * hchunk)
            hgu = jnp.dot(xb, w1_ref[0, :, cs], preferred_element_type=F32) + b1_ref[0, :, cs]
            nb = 2 * hchunk // DEINT
            hg = jnp.concatenate([hgu[:, b * DEINT:b * DEINT + q] for b in range(nb)], axis=1)
            hu = jnp.concatenate([hgu[:, b * DEINT + q:(b + 1) * DEINT] for b in range(nb)], axis=1)
            gate = jnp.minimum(hg, SWIGLU_LIMIT)
            up = jnp.clip(hu, -SWIGLU_LIMIT, SWIGLU_LIMIT)
            act = (up + 1.0) * (gate * jax.nn.sigmoid(gate * SWIGLU_ALPHA))
            y = y + jnp.dot(act.astype(BF16), w2_ref[0, j * hchunk:(j + 1) * hchunk, :],
                            preferred_element_type=F32)
        @pl.when(i >= 1)
        def _():
            wait_scatter(slot)

        for s in range(R):
            ybuf[slot, pl.ds(s, tb, stride=R), :] = y[:, s * LANES:(s + 1) * LANES]

    @pl.when(i == nused)
    def _():
        wait_scatter(slot)
        wait_gather(gslot)
        wait_gather((i + 1) % 3)

        def body(r, _):
            dst = pl.multiple_of(dstp_ref[0, 0, r], R)
            pltpu.make_async_copy(ybuf.at[other, pl.ds(pl.multiple_of(r * R, R), R)], y_hbm.at[pl.ds(dst, R)],
                                  ssem.at[other]).start()
            return 0
        lax.fori_loop(0, tb, body, 0, unroll=8)
        wait_scatter(other)


def _moe(h2t, route, wts, tb, n_real):
    R = SUBLANES
    D = R * LANES
    block_expert, nused, tok_blk, dst_shift = route
    nblk = tok_blk.shape[0]
    tok3 = tok_blk.reshape(nblk, 1, tb)
    dst3 = dst_shift.reshape(nblk + 1, 1, tb)
    smem_blk = lambda f: pl.BlockSpec((1, 1, tb), f, memory_space=pltpu.SMEM)
    F2 = wts["w1"].shape[2]
    blk = lambda i, be: be[jnp.minimum(i, nblk - 1)]
    grid_spec = pltpu.PrefetchScalarGridSpec(
        num_scalar_prefetch=2,
        grid=(nblk + 1,),
        in_specs=[smem_blk(lambda i, be, nu: (0, 0, 0)),
                  smem_blk(lambda i, be, nu: (min(1, nblk - 1), 0, 0)),
                  smem_blk(lambda i, be, nu: (jnp.minimum(i + 2, nblk - 1), 0, 0)),
                  smem_blk(lambda i, be, nu: (i, 0, 0)),
                  pl.BlockSpec(memory_space=pl.ANY),
                  pl.BlockSpec((1, D, F2), lambda i, be, nu: (blk(i, be), 0, 0)),
                  pl.BlockSpec((1, F2 // 2, D), lambda i, be, nu: (blk(i, be), 0, 0)),
                  pl.BlockSpec((1, 1, F2), lambda i, be, nu: (blk(i, be), 0, 0)),
                  pl.BlockSpec((1, 1, D), lambda i, be, nu: (blk(i, be), 0, 0))],
        out_specs=pl.BlockSpec(memory_space=pl.ANY),
        scratch_shapes=[pltpu.VMEM((3, tb * R, LANES), F32), pltpu.VMEM((2, tb * R, LANES), F32),
                        pltpu.SemaphoreType.DMA((3,)), pltpu.SemaphoreType.DMA((2,))],
    )
    return pl.pallas_call(
        functools.partial(_moe_kernel, tb=tb, hchunk=256, n_real=n_real),
        grid_spec=grid_spec,
        out_shape=jax.ShapeDtypeStruct(((n_real + 2 * tb) * R, LANES), F32),
        compiler_params=_cparams(("arbitrary",)),
        name="moe_ffn",
    )(block_expert, nused, tok3, tok3, tok3, dst3, h2t, wts["w1"], wts["w2"], wts["b1"], wts["b2"])


def _route(top_idx, tb):
    E = N_EXPERTS
    R = SUBLANES
    T = top_idx.shape[0]
    N = T * TOP_K
    assert E * N < 2 ** 31
    nblk = N // tb + E
    keys = jnp.sort(top_idx.reshape(N) * N + jnp.arange(N, dtype=I32))
    order = keys % N
    starts = jnp.searchsorted(keys, jnp.arange(E + 1, dtype=I32) * N).astype(I32)
    counts = starts[1:] - starts[:-1]
    nb = (counts + tb - 1) // tb
    bends = jnp.cumsum(nb)
    nused = bends[E - 1]
    blk = jnp.arange(nblk, dtype=I32)
    be = jnp.minimum(jnp.searchsorted(bends, jnp.minimum(blk, nused - 1), side='right'), E - 1).astype(I32)
    j = blk - (bends - nb)[be]
    first = starts[be] + j * tb
    length = jnp.where(blk < nused, jnp.clip(counts[be] - j * tb, 0, tb), 0)
    r = jnp.arange(tb, dtype=I32)
    slot_blk = order[jnp.clip(first[:, None] + r[None, :], 0, N - 1)]
    tok_blk = (slot_blk // TOP_K) * R
    dump = N + (blk % 2)[:, None] * tb + r[None, :]
    row_blk = (slot_blk % TOP_K) * T + slot_blk // TOP_K
    dst_blk = jnp.where(r[None, :] < length[:, None], row_blk, dump) * R
    dst_shift = jnp.concatenate([(N + tb + r[None, :]) * R, dst_blk], axis=0)
    return be, nused.reshape(1).astype(I32), tok_blk, dst_shift


def _combine_kernel(x_ref, g2_ref, gate_ref, y0_ref, y1_ref, y2_ref, y3_ref, fg_ref, o_ref, *, final):
    g = gate_ref[...]
    tm, D = x_ref.shape
    R = D // LANES
    cols = []
    for s in range(R):
        rows = pl.ds(s, tm, stride=R)
        cols.append(g[:, 0:1] * y0_ref[rows, :] + g[:, 1:2] * y1_ref[rows, :]
                    + g[:, 2:3] * y2_ref[rows, :] + g[:, 3:4] * y3_ref[rows, :])
    x = x_ref[...] + g2_ref[0] * jnp.concatenate(cols, axis=1)
    if final:
        x = x * lax.rsqrt(jnp.mean(x * x, axis=-1, keepdims=True) + NORM_EPS) * fg_ref[...]
    o_ref[...] = x


def _combine(x2, g2, gates, y4, final_g, B, S, tm, final):
    T, D = x2.shape
    R = D // LANES
    tps = S // tm
    nt = T // tm
    row = lambda i: (i, 0)
    yspec = lambda k: pl.BlockSpec((tm * R, LANES), lambda i, k=k: (k * nt + i, 0))
    return pl.pallas_call(
        functools.partial(_combine_kernel, final=final),
        grid=(nt,),
        in_specs=[pl.BlockSpec((tm, D), row),
                  pl.BlockSpec((1, 1, D), lambda i: (i // tps, 0, 0)),
                  pl.BlockSpec((tm, TOP_K), row),
                  yspec(0), yspec(1), yspec(2), yspec(3),
                  pl.BlockSpec((1, D), lambda i: (0, 0))],
        out_specs=pl.BlockSpec((tm, D), row),
        out_shape=jax.ShapeDtypeStruct((T, D), F32),
        compiler_params=_cparams(("arbitrary",)),
        name="combine",
    )(x2, g2, gates, y4, y4, y4, y4, final_g)


def _pack_layer(l, w_in, b_gate, mla_q_norm, mla_w_uq, mla_kv_norm, mla_w_ukv, conv_w, w_branch, w_out,
                router_w, router_b, exp_w1, exp_b1, exp_w2, exp_b2):
    D = w_in.shape[1]
    wi = w_in[l]
    o = [0]

    def seg(n):
        s = wi[:, o[0]:o[0] + n]
        o[0] += n
        return s
    q_lat, kv_full, conv = seg(MLA_Q_LORA), seg(MLA_KV_LORA + MLA_ROPE), seg(3 * CONV_DIM)
    dq, dkv, iq = seg(DSA_HEADS * DSA_HEAD_DIM), seg(2 * DSA_HEAD_DIM), seg(IDX_HEADS * IDX_DIM)
    ik, iw, gates = seg(IDX_DIM), seg(IDX_HEADS), seg(N_BRANCH * D)
    zeros = lambda n: jnp.zeros((D, n), wi.dtype)
    w_main = jnp.concatenate([
        q_lat, kv_full[:, :MLA_KV_LORA],
        kv_full[:, MLA_KV_LORA:], zeros(LANES - MLA_ROPE),
        ik, dkv[:, :DSA_HEAD_DIM],
        conv], axis=1).astype(BF16)
    assert w_main.shape[1] == MAIN_COLS
    w_t = jnp.concatenate([dq, iq, dkv[:, DSA_HEAD_DIM:], iw, zeros(T_ROWS - ROW_IW - IDX_HEADS)],
                          axis=1).T.astype(BF16)
    assert w_t.shape[0] == T_ROWS
    H = MLA_HEADS
    hq = MLA_ROPE + MLA_NOPE
    w_uq = mla_w_uq[l].reshape(MLA_Q_LORA, H, hq)
    w_uq = jnp.pad(w_uq, ((0, 0), (0, 0), (0, HEAD_PAD - hq))).transpose(1, 2, 0).astype(BF16)
    w_ukv = mla_w_ukv[l].reshape(MLA_KV_LORA, H, MLA_NOPE + MLA_V)
    w_k = jnp.pad(w_ukv[:, :, :MLA_NOPE], ((0, 0), (0, 0), (MLA_ROPE, HEAD_PAD - hq)))
    w_k = w_k.transpose(1, 0, 2).astype(BF16)
    w_v = w_ukv[:, :, MLA_NOPE:].transpose(1, 2, 0).astype(BF16)
    b1 = exp_b1[l]
    E, F2 = b1.shape
    b1p = b1.reshape(E, F2 // DEINT, DEINT // 2, 2).transpose(0, 1, 3, 2).reshape(E, 1, F2)
    return dict(
        w_main=w_main, w_t=w_t,
        q_norm=mla_q_norm[l].reshape(1, -1), kv_norm=mla_kv_norm[l].reshape(1, -1),
        w_uq=w_uq, w_k=w_k, w_v=w_v, conv_w=conv_w[l],
        w_gate=gates.astype(BF16), b_gate=b_gate[l].reshape(1, -1),
        w_branch=w_branch[l].astype(BF16), w_out=w_out[l].astype(BF16),
        router_wt=router_w[l].T.astype(BF16), router_b=router_b[l].reshape(-1, 1),
        w1=_deinterleave_w1(exp_w1, l), w2=exp_w2[l].astype(BF16),
        b1=b1p, b2=exp_b2[l].reshape(E, 1, -1),
    )


def _tiles(S, T):
    tm = min(512, S)
    tq = min(256, S)
    tk = min(256, tq)
    N = T * TOP_K
    tb = 512 if N >= 512 * N_EXPERTS else 128
    return tm, tq, tk, tb


def kernel(x, c, positions, norm1_g, norm2_g, w_ada, b_ada, w_in, b_gate, mla_q_norm, mla_w_uq, mla_kv_norm,
           mla_w_ukv, conv_w, w_branch, w_out, router_w, router_b, exp_w1, exp_b1, exp_w2, exp_b2, final_g):
    B, S, D = x.shape
    T = B * S
    L = w_ada.shape[0]
    tm, tq, tk, tb = _tiles(S, T)
    mod = _ada(c, w_ada, b_ada)
    tabs = _rope_tables(positions)
    x2 = x.reshape(T, D)
    for l in range(L):
        wts = _pack_layer(l, w_in, b_gate, mla_q_norm, mla_w_uq, mla_kv_norm, mla_w_ukv, conv_w, w_branch,
                          w_out, router_w, router_b, exp_w1, exp_b1, exp_w2, exp_b2)
        sh1, sc1, g1, sh2, sc2, g2 = [m.reshape(B, 1, D) for m in jnp.split(mod[l], 6, axis=-1)]
        n1 = norm1_g[l].reshape(1, D)
        n2 = norm2_g[l].reshape(1, D)
        q, k, v, ob, qc, qi, kk, vc, iw = _front(x2, sh1, sc1, n1, wts, tabs, B, S, tm)
        oa = _mla(q, k, v, tq, tk)
        oc = _dsa(qi, kk, iw, qc, vc, tq, tk)
        x2, h2t, idx8, gate8 = _merge(x2, (sh1, sc1, g1, sh2, sc2), n1, n2, oa, ob, oc, wts, B, S, tm)
        y4 = _moe(h2t, _route(idx8[:TOP_K].T, tb), wts, tb, T * TOP_K)
        x2 = _combine(x2, g2, gate8[:TOP_K].T, y4, final_g.reshape(1, D), B, S, tm, final=(l == L - 1))
    return x2.reshape(B, S, D)
```

```python
import functools
import math

import jax
import jax.numpy as jnp
from jax import lax
from jax.experimental import pallas as pl
from jax.experimental.pallas import tpu as pltpu

F32 = jnp.float32
BF16 = jnp.bfloat16
I32 = jnp.int32

ROPE_THETA = 500000.0
NORM_EPS = 1e-6

MLA_HEADS = 8
MLA_NOPE = 64
MLA_ROPE = 32
MLA_V = 64
MLA_Q_LORA = 256
MLA_KV_LORA = 128
CONV_DIM = 512
CONV_WIDTH = 3
DSA_HEADS = 8
DSA_HEAD_DIM = 64
DSA_ROT = 16
IDX_HEADS = 8
IDX_DIM = 64
IDX_ROT = 16
TOPK_MAX = 256
N_BRANCH = 3
BRANCH_WIDTH = 512
N_EXPERTS = 32
TOP_K = 4
SWIGLU_ALPHA = 1.702
SWIGLU_LIMIT = 7.0

LANES = 128
SUBLANES = 8
HEAD_PAD = 128
INT_MIN = -(2 ** 31)
MASK_BIAS = -1e30
LOG2E = math.log2(math.e)
VMEM_LIMIT = 56 * 1024 * 1024
NT = (((1,), (1,)), ((), ()))

SEG_QLAT = 0
SEG_KVLAT = 256
SEG_KPE = 384
SEG_KK = 512
SEG_CONV = 640
MAIN_COLS = 2176
ROW_DQ = 0
ROW_IQ = 512
ROW_VC = 1024
ROW_IW = 1088
T_ROWS = 1104
DEINT = 256


def _cparams(sem):
    return pltpu.CompilerParams(dimension_semantics=sem, vmem_limit_bytes=VMEM_LIMIT)


def _rope(x, cos, sin, half, period):
    lane = lax.broadcasted_iota(I32, x.shape, 1)
    first = (lane % period) < half
    rot = jnp.where(first, pltpu.roll(x, LANES - half, axis=1), pltpu.roll(x, half, axis=1))
    return x * cos + rot * sin


def _rope_t(x, cos, sin, half, period):
    R = x.shape[0]
    row = lax.broadcasted_iota(I32, x.shape, 0)
    first = (row % period) < half
    rot = jnp.where(first, pltpu.roll(x, R - half, axis=0), pltpu.roll(x, half, axis=0))
    reps = R // period
    if reps > 1:
        cos = jnp.tile(cos, (reps, 1))
        sin = jnp.tile(sin, (reps, 1))
    return x * cos + rot * sin


def _ada_kernel(c_ref, w_ref, b_ref, o_ref):
    c = c_ref[...]
    act = c * jax.nn.sigmoid(c)
    o_ref[0] = jnp.dot(act, w_ref[0], preferred_element_type=F32,
                       precision=lax.Precision.HIGHEST) + b_ref[0]


def _ada(c, w_ada, b_ada):
    L, D, D6 = w_ada.shape
    B = c.shape[0]
    nblk = D6 // D
    return pl.pallas_call(
        _ada_kernel,
        grid=(L, nblk),
        in_specs=[pl.BlockSpec((B, D), lambda l, j: (0, 0)),
                  pl.BlockSpec((1, D, D), lambda l, j: (l, 0, j)),
                  pl.BlockSpec((1, 1, D), lambda l, j: (l, 0, j))],
        out_specs=pl.BlockSpec((1, B, D), lambda l, j: (l, 0, j)),
        out_shape=jax.ShapeDtypeStruct((L, B, D6), F32),
        compiler_params=_cparams(("arbitrary", "arbitrary")),
        name="adaln",
    )(c, w_ada, b_ada.reshape(L, 1, D6))


def _trig_kernel(freq_ref, pos_ref, cos_ref, sin_ref):
    ang = pos_ref[...].astype(F32) * freq_ref[pl.program_id(0)]
    cos_ref[0] = jnp.cos(ang)
    sin_ref[0] = jnp.sin(ang)


def _rope_tables(positions):
    B, S = positions.shape
    T = B * S
    ha, hc = MLA_ROPE // 2, DSA_ROT // 2
    fa = jnp.exp(-math.log(ROPE_THETA) * jnp.arange(ha, dtype=F32) * (2.0 / MLA_ROPE))
    fc = jnp.exp(-math.log(ROPE_THETA) * jnp.arange(hc, dtype=F32) * (2.0 / DSA_ROT))
    freqs = jnp.concatenate([fa, fc])
    nf = ha + hc
    cos, sin = pl.pallas_call(
        _trig_kernel,
        grid=(nf,),
        in_specs=[pl.BlockSpec(memory_space=pltpu.SMEM),
                  pl.BlockSpec((B, S), lambda i: (0, 0))],
        out_specs=[pl.BlockSpec((1, B, S), lambda i: (i, 0, 0)),
                   pl.BlockSpec((1, B, S), lambda i: (i, 0, 0))],
        out_shape=[jax.ShapeDtypeStruct((nf, B, S), F32)] * 2,
        compiler_params=_cparams(("arbitrary",)),
        name="rope_trig",
    )(freqs, positions)
    cos = cos.reshape(nf, T)
    sin = sin.reshape(nf, T)
    ca, sa, cc, sc = cos[:ha], sin[:ha], cos[ha:], sin[ha:]
    cos_at = jnp.concatenate([ca, ca, jnp.ones((HEAD_PAD - 2 * ha, T), F32)], axis=0)
    sin_at = jnp.concatenate([-sa, sa, jnp.zeros((HEAD_PAD - 2 * ha, T), F32)], axis=0)
    cos_ct = jnp.concatenate([cc, cc, jnp.ones((DSA_HEAD_DIM - 2 * hc, T), F32)], axis=0)
    sin_ct = jnp.concatenate([-sc, sc, jnp.zeros((DSA_HEAD_DIM - 2 * hc, T), F32)], axis=0)
    cos_a, sin_a = cos_at.T, sin_at.T
    cos_c = jnp.concatenate([cos_ct, cos_ct], axis=0).T
    sin_c = jnp.concatenate([sin_ct, sin_ct], axis=0).T
    return dict(cos_a=cos_a, sin_a=sin_a, cos_c=cos_c, sin_c=sin_c,
                cos_at=cos_at, sin_at=sin_at, cos_ct=cos_ct, sin_ct=sin_ct)


def _front_kernel(x_ref, sh_ref, sc_ref, g_ref, w_ref, wt_ref, qn_ref, wuq_ref, kvn_ref, wk_ref, wv_ref, cw_ref,
                  ca_ref, sa_ref, cc_ref, sn_ref, cat_ref, sat_ref, cct_ref, snt_ref,
                  q_out, k_out, v_out, ob_out, qc_out, qi_out, kk_out, vc_out, iw_out,
                  carry_ref, *, tiles_per_seq):
    i = pl.program_id(0)
    x = x_ref[...]
    ms = jnp.mean(x * x, axis=-1, keepdims=True)
    h = (x * lax.rsqrt(ms + NORM_EPS) * g_ref[...]) * (1.0 + sc_ref[0]) + sh_ref[0]
    hb = h.astype(BF16)

    def proj(start, width):
        return jnp.dot(hb, w_ref[:, start:start + width], preferred_element_type=F32)

    def proj_t(start, rows):
        return lax.dot_general(wt_ref[start:start + rows, :], hb, NT, preferred_element_type=F32)

    q_lat = proj(SEG_QLAT, MLA_Q_LORA)
    cq = q_lat * lax.rsqrt(jnp.mean(q_lat * q_lat, axis=-1, keepdims=True) + NORM_EPS) * qn_ref[...]
    cqb = cq.astype(BF16)
    kv_lat = proj(SEG_KVLAT, MLA_KV_LORA)
    ckv = kv_lat * lax.rsqrt(jnp.mean(kv_lat * kv_lat, axis=-1, keepdims=True) + NORM_EPS) * kvn_ref[...]
    ckvb = ckv.astype(BF16)
    k_pe = _rope(proj(SEG_KPE, LANES), ca_ref[...], sa_ref[...], MLA_ROPE // 2, LANES)
    cat, sat = cat_ref[...], sat_ref[...]
    q_scale = (MLA_NOPE + MLA_ROPE) ** -0.5 * LOG2E
    for hd in range(MLA_HEADS):
        qh = lax.dot_general(wuq_ref[hd], cqb, NT, preferred_element_type=F32)
        q_out[0, hd] = (_rope_t(qh, cat, sat, MLA_ROPE // 2, HEAD_PAD) * q_scale).astype(BF16)
        kh = jnp.dot(ckvb, wk_ref[hd], preferred_element_type=F32) + k_pe
        k_out[0, hd] = kh.astype(BF16)
        v_out[0, hd] = lax.dot_general(wv_ref[hd], ckvb, NT, preferred_element_type=F32).astype(BF16)

    g_b = proj(SEG_CONV, CONV_DIM)
    u = proj(SEG_CONV + CONV_DIM, CONV_DIM) * proj(SEG_CONV + 2 * CONV_DIM, CONV_DIM)
    tm = u.shape[0]

    @pl.when(i % tiles_per_seq == 0)
    def _():
        carry_ref[...] = jnp.zeros(carry_ref.shape, F32)

    prev = carry_ref[...]
    row = lax.broadcasted_iota(I32, u.shape, 0)
    u1 = jnp.where(row == 0, prev[7:8], pltpu.roll(u, 1, axis=0))
    u2 = jnp.where(row == 0, prev[6:7], jnp.where(row == 1, prev[7:8], pltpu.roll(u, 2, axis=0)))
    carry_ref[...] = u[tm - SUBLANES:tm]
    cw = cw_ref[...]
    y = cw[0:1] * u2 + cw[1:2] * u1 + cw[2:3] * u
    ob_out[...] = (g_b * y).astype(BF16)

    hc = DSA_ROT // 2
    cct, snt = cct_ref[...], snt_ref[...]
    nq_rows = DSA_HEADS * DSA_HEAD_DIM
    qc_scale = DSA_HEAD_DIM ** -0.5 * LOG2E
    qc_out[0] = (_rope_t(proj_t(ROW_DQ, nq_rows), cct, snt, hc, DSA_HEAD_DIM) * qc_scale).astype(BF16)
    qi_out[0] = (_rope_t(proj_t(ROW_IQ, nq_rows), cct, snt, hc, IDX_DIM) * IDX_DIM ** -0.5).astype(BF16)
    kk_out[...] = _rope(proj(SEG_KK, LANES), cc_ref[...], sn_ref[...], hc, DSA_HEAD_DIM).astype(BF16)
    vc_out[0] = proj_t(ROW_VC, DSA_HEAD_DIM).astype(BF16)
    iw_out[0] = proj_t(ROW_IW, IDX_HEADS) * IDX_HEADS ** -0.5


def _front(x2, shift, scale, norm_g, wts, tabs, B, S, tm):
    T, D = x2.shape
    tps = S // tm
    nt = T // tm
    row = lambda i: (i, 0)
    col = lambda i: (0, i)
    bat = lambda i: (i // tps, 0, 0)
    head = lambda i: (i // tps, 0, i % tps, 0)
    head_t = lambda i: (i // tps, 0, 0, i % tps)
    seq_t = lambda i: (i // tps, 0, i % tps)
    full2 = lambda i: (0, 0)
    full3 = lambda i: (0, 0, 0)
    H = MLA_HEADS
    QR = DSA_HEADS * DSA_HEAD_DIM
    in_specs = [
        pl.BlockSpec((tm, D), row),
        pl.BlockSpec((1, 1, D), bat),
        pl.BlockSpec((1, 1, D), bat),
        pl.BlockSpec((1, D), full2),
        pl.BlockSpec((D, MAIN_COLS), full2),
        pl.BlockSpec((T_ROWS, D), full2),
        pl.BlockSpec((1, MLA_Q_LORA), full2),
        pl.BlockSpec((H, HEAD_PAD, MLA_Q_LORA), full3),
        pl.BlockSpec((1, MLA_KV_LORA), full2),
        pl.BlockSpec((H, MLA_KV_LORA, HEAD_PAD), full3),
        pl.BlockSpec((H, MLA_V, MLA_KV_LORA), full3),
        pl.BlockSpec((CONV_WIDTH, CONV_DIM), full2),
        pl.BlockSpec((tm, LANES), row),
        pl.BlockSpec((tm, LANES), row),
        pl.BlockSpec((tm, LANES), row),
        pl.BlockSpec((tm, LANES), row),
        pl.BlockSpec((HEAD_PAD, tm), col),
        pl.BlockSpec((HEAD_PAD, tm), col),
        pl.BlockSpec((DSA_HEAD_DIM, tm), col),
        pl.BlockSpec((DSA_HEAD_DIM, tm), col),
    ]
    out_specs = [
        pl.BlockSpec((1, H, HEAD_PAD, tm), head_t),
        pl.BlockSpec((1, H, tm, HEAD_PAD), head),
        pl.BlockSpec((1, H, MLA_V, tm), head_t),
        pl.BlockSpec((tm, CONV_DIM), row),
        pl.BlockSpec((1, QR, tm), seq_t),
        pl.BlockSpec((1, QR, tm), seq_t),
        pl.BlockSpec((tm, LANES), row),
        pl.BlockSpec((1, DSA_HEAD_DIM, tm), seq_t),
        pl.BlockSpec((1, IDX_HEADS, tm), seq_t),
    ]
    out_shape = [
        jax.ShapeDtypeStruct((B, H, HEAD_PAD, S), BF16),
        jax.ShapeDtypeStruct((B, H, S, HEAD_PAD), BF16),
        jax.ShapeDtypeStruct((B, H, MLA_V, S), BF16),
        jax.ShapeDtypeStruct((T, CONV_DIM), BF16),
        jax.ShapeDtypeStruct((B, QR, S), BF16),
        jax.ShapeDtypeStruct((B, QR, S), BF16),
        jax.ShapeDtypeStruct((T, LANES), BF16),
        jax.ShapeDtypeStruct((B, DSA_HEAD_DIM, S), BF16),
        jax.ShapeDtypeStruct((B, IDX_HEADS, S), F32),
    ]
    return pl.pallas_call(
        functools.partial(_front_kernel, tiles_per_seq=tps),
        grid=(nt,),
        in_specs=in_specs,
        out_specs=out_specs,
        out_shape=out_shape,
        scratch_shapes=[pltpu.VMEM((SUBLANES, CONV_DIM), F32)],
        compiler_params=_cparams(("arbitrary",)),
        name="front",
    )(x2, shift, scale, norm_g, wts["w_main"], wts["w_t"], wts["q_norm"], wts["w_uq"], wts["kv_norm"],
      wts["w_k"], wts["w_v"], wts["conv_w"],
      tabs["cos_a"], tabs["sin_a"], tabs["cos_c"], tabs["sin_c"],
      tabs["cos_at"], tabs["sin_at"], tabs["cos_ct"], tabs["sin_ct"])


def _attend_heads(key_tile, q_tile, v_tile, bias, m_ref, l_ref, acc_ref, nheads, vdim):
    scores = [jnp.dot(key_tile(hd), q_tile(hd), preferred_element_type=F32) for hd in range(nheads)]
    probs, alphas = [], []
    for hd in range(nheads):
        s = scores[hd] if bias is None else scores[hd] + bias
        m_new = jnp.maximum(m_ref[hd], jnp.max(s, axis=0, keepdims=True))
        alpha = jnp.exp2(m_ref[hd] - m_new)
        p = jnp.exp2(s - m_new[0:1])
        l_ref[hd] = alpha * l_ref[hd] + jnp.sum(p, axis=0, keepdims=True)
        m_ref[hd] = m_new
        probs.append(p.astype(BF16))
        alphas.append(alpha[0:1])
    for hd in range(nheads):
        rows = slice(hd * vdim, (hd + 1) * vdim)
        acc_ref[rows, :] = alphas[hd] * acc_ref[rows, :] + jnp.dot(v_tile(hd), probs[hd],
                                                                   preferred_element_type=F32)


def _attend_init(m_ref, l_ref, acc_ref):
    m_ref[...] = jnp.full(m_ref.shape, MASK_BIAS, F32)
    l_ref[...] = jnp.zeros(l_ref.shape, F32)
    acc_ref[...] = jnp.zeros(acc_ref.shape, F32)


def _attend_finish(o_ref, l_ref, acc_ref, nheads, vdim):
    outs = [acc_ref[hd * vdim:(hd + 1) * vdim, :] / l_ref[hd, 0:1, :] for hd in range(nheads)]
    o_ref[...] = jnp.concatenate(outs, axis=0).T.astype(BF16)


def _mla_kernel(q_ref, k_ref, v_ref, o_ref, m_ref, l_ref, acc_ref, *, tq, tk):
    i = pl.program_id(1)
    sub = tq // tk
    _attend_init(m_ref, l_ref, acc_ref)

    def chunk(c, bias):
        off = pl.multiple_of(c * tk, tk)
        _attend_heads(lambda hd: k_ref[0, hd, pl.ds(off, tk), :],
                      lambda hd: q_ref[0, hd],
                      lambda hd: v_ref[0, hd, :, pl.ds(off, tk)],
                      bias, m_ref, l_ref, acc_ref, MLA_HEADS, MLA_V)

    def body(c, _):
        chunk(c, None)
        return 0

    lax.fori_loop(0, i * sub, body, 0)
    kpos = lax.broadcasted_iota(I32, (tk, tq), 0)
    qpos = lax.broadcasted_iota(I32, (tk, tq), 1)
    for j in range(sub):
        chunk(i * sub + j, jnp.where(j * tk + kpos <= qpos, 0.0, MASK_BIAS))
    _attend_finish(o_ref, l_ref, acc_ref, MLA_HEADS, MLA_V)


def _mla(q, k, v, tq, tk):
    B, H, S, _ = k.shape
    nq = S // tq
    return pl.pallas_call(
        functools.partial(_mla_kernel, tq=tq, tk=tk),
        grid=(B, nq),
        in_specs=[pl.BlockSpec((1, H, HEAD_PAD, tq), lambda b, i: (b, 0, 0, i)),
                  pl.BlockSpec((1, H, S, HEAD_PAD), lambda b, i: (b, 0, 0, 0)),
                  pl.BlockSpec((1, H, MLA_V, S), lambda b, i: (b, 0, 0, 0))],
        out_specs=pl.BlockSpec((tq, H * MLA_V), lambda b, i: (b * nq + i, 0)),
        out_shape=jax.ShapeDtypeStruct((B * S, H * MLA_V), BF16),
        scratch_shapes=[pltpu.VMEM((H, SUBLANES, tq), F32), pltpu.VMEM((H, SUBLANES, tq), F32),
                        pltpu.VMEM((H * MLA_V, tq), F32)],
        compiler_params=_cparams(("arbitrary", "arbitrary")),
        name="mla_attention",
    )(q, k, v)


def _float_key(score):
    bits = pltpu.bitcast(score + 0.0, I32)
    return jnp.where(bits >= 0, bits, bits ^ jnp.int32(0x7FFFFFFF))


def _dsa_kernel(qi_ref, kk_ref, iw_ref, qc_ref, vc_ref, o_ref,
                key_ref, cut_ref, m_ref, l_ref, acc_ref, *, tq, tk, topk, seq):
    i = pl.program_id(1)
    nch = (i + 1) * (tq // tk)
    q0 = i * tq
    row = lax.broadcasted_iota(I32, (tk, tq), 0)
    qpos = q0 + lax.broadcasted_iota(I32, (tk, tq), 1)
    HD = DSA_HEAD_DIM

    iw = iw_ref[0]

    def score_chunk(c, _):
        off = pl.multiple_of(c * tk, tk)
        kic = kk_ref[pl.ds(off, tk), :][:, :IDX_DIM]
        score = jnp.zeros((tk, tq), F32)
        for hd in range(IDX_HEADS):
            lg = jnp.dot(kic, qi_ref[0, hd * IDX_DIM:(hd + 1) * IDX_DIM, :], preferred_element_type=F32)
            score = score + jnp.maximum(lg, 0.0) * iw[hd:hd + 1, :]
        key_ref[pl.ds(off, tk), :] = jnp.where(off + row <= qpos, _float_key(score), INT_MIN)
        return 0

    lax.fori_loop(0, nch, score_chunk, 0)

    def count(pred):
        def body(c, part):
            off = pl.multiple_of(c * tk, tk)
            hit = jnp.where(pred(key_ref[pl.ds(off, tk), :], off), 1.0, 0.0)
            return part + jnp.sum(hit.reshape(tk // SUBLANES, SUBLANES, tq), axis=0)
        part = lax.fori_loop(0, nch, body, jnp.zeros((SUBLANES, tq), F32))
        return jnp.sum(part, axis=0, keepdims=True)

    def search(it, t):
        cand = t + (jnp.int32(1) << (31 - it))
        cnt = count(lambda kc, off: kc >= cand)
        return jnp.where(cnt >= topk, cand, t)

    thr = lax.fori_loop(0, 32, search, jnp.full((1, tq), INT_MIN, I32))

    n_gt = count(lambda kc, off: kc > thr)
    n_eq = count(lambda kc, off: kc == thr)
    need = topk - n_gt
    tie = (thr > INT_MIN) & (n_eq > need)
    cut_ref[...] = jnp.full(cut_ref.shape, seq, I32)

    @pl.when(jnp.max(jnp.where(tie, 1.0, 0.0)) > 0.0)
    def _():
        nbits = max(1, (seq - 1).bit_length())

        def find(it, p):
            cand = p + (jnp.int32(1) << (nbits - 1 - it))
            cnt = count(lambda kc, off: (kc == thr) & (off + row < cand))
            return jnp.where(cnt < need, cand, p)

        cut = lax.fori_loop(0, nbits, find, jnp.zeros((1, tq), I32))
        cut_ref[...] = jnp.broadcast_to(jnp.where(tie, cut, seq), cut_ref.shape)

    _attend_init(m_ref, l_ref, acc_ref)
    cut1 = cut_ref[0:1, :]

    def attend(c, _):
        off = pl.multiple_of(c * tk, tk)
        key = key_ref[pl.ds(off, tk), :]
        kpos = off + row
        sel = ((key > thr) | ((key == thr) & (kpos <= cut1))) & (kpos <= qpos)
        bias = jnp.where(sel, 0.0, MASK_BIAS)
        kc = kk_ref[pl.ds(off, tk), :][:, IDX_DIM:]
        vt = vc_ref[0, :, pl.ds(off, tk)]
        _attend_heads(lambda hd: kc, lambda hd: qc_ref[0, hd * HD:(hd + 1) * HD, :], lambda hd: vt,
                      bias, m_ref, l_ref, acc_ref, DSA_HEADS, HD)
        return 0

    lax.fori_loop(0, nch, attend, 0)
    _attend_finish(o_ref, l_ref, acc_ref, DSA_HEADS, HD)


def _dsa(qi, kk, iw, qc, vc, tq, tk):
    B, QR, S = qc.shape
    nq = S // tq
    topk = min(TOPK_MAX, S // 4)
    return pl.pallas_call(
        functools.partial(_dsa_kernel, tq=tq, tk=tk, topk=topk, seq=S),
        grid=(B, nq),
        in_specs=[pl.BlockSpec((1, QR, tq), lambda b, i: (b, 0, i)),
                  pl.BlockSpec((S, LANES), lambda b, i: (b, 0)),
                  pl.BlockSpec((1, IDX_HEADS, tq), lambda b, i: (b, 0, i)),
                  pl.BlockSpec((1, QR, tq), lambda b, i: (b, 0, i)),
                  pl.BlockSpec((1, DSA_HEAD_DIM, S), lambda b, i: (b, 0, 0))],
        out_specs=pl.BlockSpec((tq, QR), lambda b, i: (b * nq + i, 0)),
        out_shape=jax.ShapeDtypeStruct((B * S, QR), BF16),
        scratch_shapes=[pltpu.VMEM((S, tq), I32),
                        pltpu.VMEM((SUBLANES, tq), I32),
                        pltpu.VMEM((DSA_HEADS, SUBLANES, tq), F32),
                        pltpu.VMEM((DSA_HEADS, SUBLANES, tq), F32),
                        pltpu.VMEM((QR, tq), F32)],
        compiler_params=_cparams(("arbitrary", "arbitrary")),
        name="dsa_attention",
    )(qi, kk, iw, qc, vc)


def _merge_kernel(x_ref, sh1_ref, sc1_ref, g1_ref, sh2_ref, sc2_ref, n1_ref, n2_ref,
                  oa_ref, ob_ref, oc_ref, wg_ref, bg_ref, wb_ref, wo_ref, rw_ref, rb_ref,
                  xo_ref, h2_ref, idx_ref, gate_ref):
    x = x_ref[...]
    tm, D = x.shape
    ms = jnp.mean(x * x, axis=-1, keepdims=True)
    h = (x * lax.rsqrt(ms + NORM_EPS) * n1_ref[...]) * (1.0 + sc1_ref[0]) + sh1_ref[0]
    hb = h.astype(BF16)
    mix = jnp.zeros(x.shape, F32)
    for n, o_ref in enumerate((oa_ref, ob_ref, oc_ref)):
        y = jnp.dot(o_ref[...], wb_ref[n], preferred_element_type=F32)
        gl = jnp.dot(hb, wg_ref[:, n * D:(n + 1) * D], preferred_element_type=F32) + bg_ref[:, n * D:(n + 1) * D]
        mix = mix + jax.nn.sigmoid(gl) * y
    out = jnp.dot(mix.astype(BF16), wo_ref[...], preferred_element_type=F32)
    xn = x + g1_ref[0] * out
    xo_ref[...] = xn
    ms2 = jnp.mean(xn * xn, axis=-1, keepdims=True)
    h2 = (xn * lax.rsqrt(ms2 + NORM_EPS) * n2_ref[...]) * (1.0 + sc2_ref[0]) + sh2_ref[0]
    for s in range(D // LANES):
        h2_ref[pl.ds(s, tm, stride=D // LANES), :] = h2[:, s * LANES:(s + 1) * LANES]
    logits = lax.dot_general(rw_ref[...], h2.astype(BF16), NT, preferred_element_type=F32) + rb_ref[...]
    eidx = lax.broadcasted_iota(I32, logits.shape, 0)
    vals, idxs = [], []
    for _ in range(TOP_K):
        mx = jnp.max(logits, axis=0, keepdims=True)
        am = jnp.min(jnp.where(logits == mx, eidx, N_EXPERTS), axis=0, keepdims=True)
        vals.append(mx)
        idxs.append(am)
        logits = jnp.where(eidx == am, -jnp.inf, logits)
    ex = [jnp.exp(v - vals[0]) for v in vals]
    den = ex[0] + ex[1] + ex[2] + ex[3]
    pad_i = [jnp.zeros_like(idxs[0])] * (SUBLANES - TOP_K)
    pad_f = [jnp.zeros_like(den)] * (SUBLANES - TOP_K)
    idx_ref[...] = jnp.concatenate(idxs + pad_i, axis=0)
    gate_ref[...] = jnp.concatenate([e / den for e in ex] + pad_f, axis=0)


def _merge(x2, mods, n1, n2, oa, ob, oc, wts, B, S, tm):
    T, D = x2.shape
    tps = S // tm
    R = D // LANES
    row = lambda i: (i, 0)
    bat = lambda i: (i // tps, 0, 0)
    full2 = lambda i: (0, 0)
    full3 = lambda i: (0, 0, 0)
    mod_spec = pl.BlockSpec((1, 1, D), bat)
    W = BRANCH_WIDTH
    return pl.pallas_call(
        _merge_kernel,
        grid=(T // tm,),
        in_specs=[pl.BlockSpec((tm, D), row)] + [mod_spec] * 5 + [
            pl.BlockSpec((1, D), full2), pl.BlockSpec((1, D), full2),
            pl.BlockSpec((tm, W), row), pl.BlockSpec((tm, W), row), pl.BlockSpec((tm, W), row),
            pl.BlockSpec((D, N_BRANCH * D), full2), pl.BlockSpec((1, N_BRANCH * D), full2),
            pl.BlockSpec((N_BRANCH, W, D), full3), pl.BlockSpec((D, D), full2),
            pl.BlockSpec((N_EXPERTS, D), full2), pl.BlockSpec((N_EXPERTS, 1), full2)],
        out_specs=[pl.BlockSpec((tm, D), row), pl.BlockSpec((tm * R, LANES), row),
                   pl.BlockSpec((SUBLANES, tm), lambda i: (0, i)), pl.BlockSpec((SUBLANES, tm), lambda i: (0, i))],
        out_shape=[jax.ShapeDtypeStruct((T, D), F32), jax.ShapeDtypeStruct((T * R, LANES), F32),
                   jax.ShapeDtypeStruct((SUBLANES, T), I32), jax.ShapeDtypeStruct((SUBLANES, T), F32)],
        compiler_params=_cparams(("arbitrary",)),
        name="merge_router",
    )(x2, *mods, n1, n2, oa, ob, oc, wts["w_gate"], wts["b_gate"], wts["w_branch"], wts["w_out"],
      wts["router_wt"], wts["router_b"])


def _deint_kernel(w_ref, p_ref, o_ref):
    o_ref[0] = jnp.dot(w_ref[0, 0].astype(BF16), p_ref[...], preferred_element_type=F32).astype(BF16)


def _deinterleave_w1(exp_w1, l):
    _, E, D, F2 = exp_w1.shape
    half = DEINT // 2
    j = jnp.arange(half)
    perm = jnp.zeros((DEINT, DEINT), BF16).at[2 * j, j].set(1.0).at[2 * j + 1, half + j].set(1.0)
    return pl.pallas_call(
        _deint_kernel,
        grid=(E, F2 // DEINT),
        in_specs=[pl.BlockSpec((1, 1, D, DEINT), lambda e, b: (l, e, 0, b)),
                  pl.BlockSpec((DEINT, DEINT), lambda e, b: (0, 0))],
        out_specs=pl.BlockSpec((1, D, DEINT), lambda e, b: (e, 0, b)),
        out_shape=jax.ShapeDtypeStruct((E, D, F2), BF16),
        compiler_params=_cparams(("arbitrary", "arbitrary")),
        name="w1_deinterleave",
    )(exp_w1, perm)


def _moe_kernel(be_ref, nu_ref, tok0_ref, tok1_ref, tokn_ref, dstp_ref, x_hbm, w1_ref, w2_ref, b1_ref, b2_ref,
                y_hbm, xbuf, ybuf, gsem, ssem, *, tb, hchunk, n_real):
    i = pl.program_id(0)
    nused = nu_ref[0]
    slot = i % 2
    other = 1 - slot
    gslot = i % 3
    R = SUBLANES

    def gather_copy(tok_ref, s, r):
        src = pl.multiple_of(tok_ref[0, 0, r], R)
        return pltpu.make_async_copy(x_hbm.at[pl.ds(src, R)], xbuf.at[s, pl.ds(r * R, R)], gsem.at[s])

    def scatter_copy(s, r):
        dst = pl.multiple_of(dstp_ref[0, 0, r], R)
        return pltpu.make_async_copy(ybuf.at[s, pl.ds(r * R, R)], y_hbm.at[pl.ds(dst, R)], ssem.at[s])

    def wait_gather(s):
        pltpu.make_async_copy(x_hbm.at[pl.ds(0, tb * R)], xbuf.at[s], gsem.at[s]).wait()

    def wait_scatter(s):
        pltpu.make_async_copy(ybuf.at[s], y_hbm.at[pl.ds(0, tb * R)], ssem.at[s]).wait()

    @pl.when(i == 0)
    def _():
        ybuf[...] = jnp.zeros(ybuf.shape, F32)
        for s in range(2):
            fill = pltpu.make_async_copy(ybuf.at[s], y_hbm.at[pl.ds((n_real + s * tb) * R, tb * R)], ssem.at[s])
            fill.start()
            fill.wait()

        def body(r, _):
            for s, tok_ref in enumerate((tok0_ref, tok1_ref)):
                src = pl.multiple_of(tok_ref[0, 0, r], R)
                pltpu.make_async_copy(x_hbm.at[pl.ds(src, R)], xbuf.at[s, pl.ds(pl.multiple_of(r * R, R), R)],
                                      gsem.at[s]).start()
            return 0
        lax.fori_loop(0, tb, body, 0, unroll=8)

    @pl.when(i < nused)
    def _():
        wait_gather(gslot)
        D = R * LANES
        xb = jnp.concatenate([xbuf[gslot, pl.ds(s, tb, stride=R), :] for s in range(R)], axis=1).astype(BF16)
        ahead = (i + 2) % 3
        for r in range(tb):
            scatter_copy(other, r).start(priority=r % 2)
            gather_copy(tokn_ref, ahead, r).start(priority=(r + 1) % 2)
        y = jnp.zeros((tb, D), F32) + b2_ref[0]
        q = DEINT // 2
        for j in range(D // hchunk):
            cs = slice(2 * j * hchunk, 2 * (j + 1) * hchunk)
            hgu = jnp.dot(xb, w1_ref[0, :, cs], preferred_element_type=F32) + b1_ref[0, :, cs]
            nb = 2 * hchunk // DEINT
            hg = jnp.concatenate([hgu[:, b * DEINT:b * DEINT + q] for b in range(nb)], axis=1)
            hu = jnp.concatenate([hgu[:, b * DEINT + q:(b + 1) * DEINT] for b in range(nb)], axis=1)
            gate = jnp.minimum(hg, SWIGLU_LIMIT)
            up = jnp.clip(hu, -SWIGLU_LIMIT, SWIGLU_LIMIT)
            act = (up + 1.0) * (gate * jax.nn.sigmoid(gate * SWIGLU_ALPHA))
            y = y + jnp.dot(act.astype(BF16), w2_ref[0, j * hchunk:(j + 1) * hchunk, :],
                            preferred_element_type=F32)
        @pl.when(i >= 1)
        def _():
            wait_scatter(slot)

        for s in range(R):
            ybuf[slot, pl.ds(s, tb, stride=R), :] = y[:, s * LANES:(s + 1) * LANES]

    @pl.when(i == nused)
    def _():
        wait_scatter(slot)
        wait_gather(gslot)
        wait_gather((i + 1) % 3)

        def body(r, _):
            dst = pl.multiple_of(dstp_ref[0, 0, r], R)
            pltpu.make_async_copy(ybuf.at[other, pl.ds(pl.multiple_of(r * R, R), R)], y_hbm.at[pl.ds(dst, R)],
                                  ssem.at[other]).start()
            return 0
        lax.fori_loop(0, tb, body, 0, unroll=8)
        wait_scatter(other)


def _moe(h2t, route, wts, tb, n_real):
    R = SUBLANES
    D = R * LANES
    block_expert, nused, tok_blk, dst_shift = route
    nblk = tok_blk.shape[0]
    tok3 = tok_blk.reshape(nblk, 1, tb)
    dst3 = dst_shift.reshape(nblk + 1, 1, tb)
    smem_blk = lambda f: pl.BlockSpec((1, 1, tb), f, memory_space=pltpu.SMEM)
    F2 = wts["w1"].shape[2]
    blk = lambda i, be: be[jnp.minimum(i, nblk - 1)]
    grid_spec = pltpu.PrefetchScalarGridSpec(
        num_scalar_prefetch=2,
        grid=(nblk + 1,),
        in_specs=[smem_blk(lambda i, be, nu: (0, 0, 0)),
                  smem_blk(lambda i, be, nu: (min(1, nblk - 1), 0, 0)),
                  smem_blk(lambda i, be, nu: (jnp.minimum(i + 2, nblk - 1), 0, 0)),
                  smem_blk(lambda i, be, nu: (i, 0, 0)),
                  pl.BlockSpec(memory_space=pl.ANY),
                  pl.BlockSpec((1, D, F2), lambda i, be, nu: (blk(i, be), 0, 0)),
                  pl.BlockSpec((1, F2 // 2, D), lambda i, be, nu: (blk(i, be), 0, 0)),
                  pl.BlockSpec((1, 1, F2), lambda i, be, nu: (blk(i, be), 0, 0)),
                  pl.BlockSpec((1, 1, D), lambda i, be, nu: (blk(i, be), 0, 0))],
        out_specs=pl.BlockSpec(memory_space=pl.ANY),
        scratch_shapes=[pltpu.VMEM((3, tb * R, LANES), F32), pltpu.VMEM((2, tb * R, LANES), F32),
                        pltpu.SemaphoreType.DMA((3,)), pltpu.SemaphoreType.DMA((2,))],
    )
    return pl.pallas_call(
        functools.partial(_moe_kernel, tb=tb, hchunk=256, n_real=n_real),
        grid_spec=grid_spec,
        out_shape=jax.ShapeDtypeStruct(((n_real + 2 * tb) * R, LANES), F32),
        compiler_params=_cparams(("arbitrary",)),
        name="moe_ffn",
    )(block_expert, nused, tok3, tok3, tok3, dst3, h2t, wts["w1"], wts["w2"], wts["b1"], wts["b2"])


def _route(top_idx, tb):
    E = N_EXPERTS
    R = SUBLANES
    T = top_idx.shape[0]
    N = T * TOP_K
    assert E * N < 2 ** 31
    nblk = N // tb + E
    keys = jnp.sort(top_idx.reshape(N) * N + jnp.arange(N, dtype=I32))
    order = keys % N
    starts = jnp.searchsorted(keys, jnp.arange(E + 1, dtype=I32) * N).astype(I32)
    counts = starts[1:] - starts[:-1]
    nb = (counts + tb - 1) // tb
    bends = jnp.cumsum(nb)
    nused = bends[E - 1]
    blk = jnp.arange(nblk, dtype=I32)
    be = jnp.minimum(jnp.searchsorted(bends, jnp.minimum(blk, nused - 1), side='right'), E - 1).astype(I32)
    j = blk - (bends - nb)[be]
    first = starts[be] + j * tb
    length = jnp.where(blk < nused, jnp.clip(counts[be] - j * tb, 0, tb), 0)
    r = jnp.arange(tb, dtype=I32)
    slot_blk = order[jnp.clip(first[:, None] + r[None, :], 0, N - 1)]
    tok_blk = (slot_blk // TOP_K) * R
    dump = N + (blk % 2)[:, None] * tb + r[None, :]
    row_blk = (slot_blk % TOP_K) * T + slot_blk // TOP_K
    dst_blk = jnp.where(r[None, :] < length[:, None], row_blk, dump) * R
    dst_shift = jnp.concatenate([(N + tb + r[None, :]) * R, dst_blk], axis=0)
    return be, nused.reshape(1).astype(I32), tok_blk, dst_shift


def _combine_kernel(x_ref, g2_ref, gate_ref, y0_ref, y1_ref, y2_ref, y3_ref, fg_ref, o_ref, *, final):
    g = gate_ref[...]
    tm, D = x_ref.shape
    R = D // LANES
    cols = []
    for s in range(R):
        rows = pl.ds(s, tm, stride=R)
        cols.append(g[:, 0:1] * y0_ref[rows, :] + g[:, 1:2] * y1_ref[rows, :]
                    + g[:, 2:3] * y2_ref[rows, :] + g[:, 3:4] * y3_ref[rows, :])
    x = x_ref[...] + g2_ref[0] * jnp.concatenate(cols, axis=1)
    if final:
        x = x * lax.rsqrt(jnp.mean(x * x, axis=-1, keepdims=True) + NORM_EPS) * fg_ref[...]
    o_ref[...] = x


def _combine(x2, g2, gates, y4, final_g, B, S, tm, final):
    T, D = x2.shape
    R = D // LANES
    tps = S // tm
    nt = T // tm
    row = lambda i: (i, 0)
    yspec = lambda k: pl.BlockSpec((tm * R, LANES), lambda i, k=k: (k * nt + i, 0))
    return pl.pallas_call(
        functools.partial(_combine_kernel, final=final),
        grid=(nt,),
        in_specs=[pl.BlockSpec((tm, D), row),
                  pl.BlockSpec((1, 1, D), lambda i: (i // tps, 0, 0)),
                  pl.BlockSpec((tm, TOP_K), row),
                  yspec(0), yspec(1), yspec(2), yspec(3),
                  pl.BlockSpec((1, D), lambda i: (0, 0))],
        out_specs=pl.BlockSpec((tm, D), row),
        out_shape=jax.ShapeDtypeStruct((T, D), F32),
        compiler_params=_cparams(("arbitrary",)),
        name="combine",
    )(x2, g2, gates, y4, y4, y4, y4, final_g)


def _pack_layer(l, w_in, b_gate, mla_q_norm, mla_w_uq, mla_kv_norm, mla_w_ukv, conv_w, w_branch, w_out,
                router_w, router_b, exp_w1, exp_b1, exp_w2, exp_b2):
    D = w_in.shape[1]
    wi = w_in[l]
    o = [0]

    def seg(n):
        s = wi[:, o[0]:o[0] + n]
        o[0] += n
        return s
    q_lat, kv_full, conv = seg(MLA_Q_LORA), seg(MLA_KV_LORA + MLA_ROPE), seg(3 * CONV_DIM)
    dq, dkv, iq = seg(DSA_HEADS * DSA_HEAD_DIM), seg(2 * DSA_HEAD_DIM), seg(IDX_HEADS * IDX_DIM)
    ik, iw, gates = seg(IDX_DIM), seg(IDX_HEADS), seg(N_BRANCH * D)
    zeros = lambda n: jnp.zeros((D, n), wi.dtype)
    w_main = jnp.concatenate([
        q_lat, kv_full[:, :MLA_KV_LORA],
        kv_full[:, MLA_KV_LORA:], zeros(LANES - MLA_ROPE),
        ik, dkv[:, :DSA_HEAD_DIM],
        conv], axis=1).astype(BF16)
    assert w_main.shape[1] == MAIN_COLS
    w_t = jnp.concatenate([dq, iq, dkv[:, DSA_HEAD_DIM:], iw, zeros(T_ROWS - ROW_IW - IDX_HEADS)],
                          axis=1).T.astype(BF16)
    assert w_t.shape[0] == T_ROWS
    H = MLA_HEADS
    hq = MLA_ROPE + MLA_NOPE
    w_uq = mla_w_uq[l].reshape(MLA_Q_LORA, H, hq)
    w_uq = jnp.pad(w_uq, ((0, 0), (0, 0), (0, HEAD_PAD - hq))).transpose(1, 2, 0).astype(BF16)
    w_ukv = mla_w_ukv[l].reshape(MLA_KV_LORA, H, MLA_NOPE + MLA_V)
    w_k = jnp.pad(w_ukv[:, :, :MLA_NOPE], ((0, 0), (0, 0), (MLA_ROPE, HEAD_PAD - hq)))
    w_k = w_k.transpose(1, 0, 2).astype(BF16)
    w_v = w_ukv[:, :, MLA_NOPE:].transpose(1, 2, 0).astype(BF16)
    b1 = exp_b1[l]
    E, F2 = b1.shape
    b1p = b1.reshape(E, F2 // DEINT, DEINT // 2, 2).transpose(0, 1, 3, 2).reshape(E, 1, F2)
    return dict(
        w_main=w_main, w_t=w_t,
        q_norm=mla_q_norm[l].reshape(1, -1), kv_norm=mla_kv_norm[l].reshape(1, -1),
        w_uq=w_uq, w_k=w_k, w_v=w_v, conv_w=conv_w[l],
        w_gate=gates.astype(BF16), b_gate=b_gate[l].reshape(1, -1),
        w_branch=w_branch[l].astype(BF16), w_out=w_out[l].astype(BF16),
        router_wt=router_w[l].T.astype(BF16), router_b=router_b[l].reshape(-1, 1),
        w1=_deinterleave_w1(exp_w1, l), w2=exp_w2[l].astype(BF16),
        b1=b1p, b2=exp_b2[l].reshape(E, 1, -1),
    )


def _tiles(S, T):
    tm = min(512, S)
    tq = min(256, S)
    tk = min(256, tq)
    N = T * TOP_K
    tb = 512 if N >= 512 * N_EXPERTS else 128
    return tm, tq, tk, tb


def kernel(x, c, positions, norm1_g, norm2_g, w_ada, b_ada, w_in, b_gate, mla_q_norm, mla_w_uq, mla_kv_norm,
           mla_w_ukv, conv_w, w_branch, w_out, router_w, router_b, exp_w1, exp_b1, exp_w2, exp_b2, final_g):
    B, S, D = x.shape
    T = B * S
    L = w_ada.shape[0]
    tm, tq, tk, tb = _tiles(S, T)
    mod = _ada(c, w_ada, b_ada)
    tabs = _rope_tables(positions)
    x2 = x.reshape(T, D)
    for l in range(L):
        wts = _pack_layer(l, w_in, b_gate, mla_q_norm, mla_w_uq, mla_kv_norm, mla_w_ukv, conv_w, w_branch,
                          w_out, router_w, router_b, exp_w1, exp_b1, exp_w2, exp_b2)
        sh1, sc1, g1, sh2, sc2, g2 = [m.reshape(B, 1, D) for m in jnp.split(mod[l], 6, axis=-1)]
        n1 = norm1_g[l].reshape(1, D)
        n2 = norm2_g[l].reshape(1, D)
        q, k, v, ob, qc, qi, kk, vc, iw = _front(x2, sh1, sc1, n1, wts, tabs, B, S, tm)
        oa = _mla(q, k, v, min(2 * tq, S), tk)
        oc = _dsa(qi, kk, iw, qc, vc, tq, tk)
        x2, h2t, idx8, gate8 = _merge(x2, (sh1, sc1, g1, sh2, sc2), n1, n2, oa, ob, oc, wts, B, S, tm)
        y4 = _moe(h2t, _route(idx8[:TOP_K].T, tb), wts, tb, T * TOP_K)
        x2 = _combine(x2, g2, gate8[:TOP_K].T, y4, final_g.reshape(1, D), B, S, tm, final=(l == L - 1))
    return x2.reshape(B, S, D)
```

```python
import functools
import math

import jax
import jax.numpy as jnp
from jax import lax
from jax.experimental import pallas as pl
from jax.experimental.pallas import tpu as pltpu

F32 = jnp.float32
BF16 = jnp.bfloat16
I32 = jnp.int32

ROPE_THETA = 500000.0
NORM_EPS = 1e-6

MLA_HEADS = 8
MLA_NOPE = 64
MLA_ROPE = 32
MLA_V = 64
MLA_Q_LORA = 256
MLA_KV_LORA = 128
CONV_DIM = 512
CONV_WIDTH = 3
DSA_HEADS = 8
DSA_HEAD_DIM = 64
DSA_ROT = 16
IDX_HEADS = 8
IDX_DIM = 64
IDX_ROT = 16
TOPK_MAX = 256
N_BRANCH = 3
BRANCH_WIDTH = 512
N_EXPERTS = 32
TOP_K = 4
SWIGLU_ALPHA = 1.702
SWIGLU_LIMIT = 7.0

LANES = 128
SUBLANES = 8
HEAD_PAD = 128
INT_MIN = -(2 ** 31)
MASK_BIAS = -1e30
LOG2E = math.log2(math.e)
VMEM_LIMIT = 56 * 1024 * 1024
NT = (((1,), (1,)), ((), ()))

SEG_QLAT = 0
SEG_KVLAT = 256
SEG_KPE = 384
SEG_KK = 512
SEG_CONV = 640
MAIN_COLS = 2176
ROW_DQ = 0
ROW_IQ = 512
ROW_VC = 1024
ROW_IW = 1088
T_ROWS = 1104
DEINT = 256


def _cparams(sem):
    return pltpu.CompilerParams(dimension_semantics=sem, vmem_limit_bytes=VMEM_LIMIT)


def _rope(x, cos, sin, half, period):
    lane = lax.broadcasted_iota(I32, x.shape, 1)
    first = (lane % period) < half
    rot = jnp.where(first, pltpu.roll(x, LANES - half, axis=1), pltpu.roll(x, half, axis=1))
    return x * cos + rot * sin


def _rope_t(x, cos, sin, half, period):
    R = x.shape[0]
    row = lax.broadcasted_iota(I32, x.shape, 0)
    first = (row % period) < half
    rot = jnp.where(first, pltpu.roll(x, R - half, axis=0), pltpu.roll(x, half, axis=0))
    reps = R // period
    if reps > 1:
        cos = jnp.tile(cos, (reps, 1))
        sin = jnp.tile(sin, (reps, 1))
    return x * cos + rot * sin


def _ada_kernel(c_ref, w_ref, b_ref, o_ref):
    c = c_ref[...]
    act = c * jax.nn.sigmoid(c)
    o_ref[0] = jnp.dot(act, w_ref[0], preferred_element_type=F32,
                       precision=lax.Precision.HIGHEST) + b_ref[0]


def _ada(c, w_ada, b_ada):
    L, D, D6 = w_ada.shape
    B = c.shape[0]
    nblk = D6 // D
    return pl.pallas_call(
        _ada_kernel,
        grid=(L, nblk),
        in_specs=[pl.BlockSpec((B, D), lambda l, j: (0, 0)),
                  pl.BlockSpec((1, D, D), lambda l, j: (l, 0, j)),
                  pl.BlockSpec((1, 1, D), lambda l, j: (l, 0, j))],
        out_specs=pl.BlockSpec((1, B, D), lambda l, j: (l, 0, j)),
        out_shape=jax.ShapeDtypeStruct((L, B, D6), F32),
        compiler_params=_cparams(("arbitrary", "arbitrary")),
        name="adaln",
    )(c, w_ada, b_ada.reshape(L, 1, D6))


def _trig_kernel(freq_ref, pos_ref, cos_ref, sin_ref):
    ang = pos_ref[...].astype(F32) * freq_ref[pl.program_id(0)]
    cos_ref[0] = jnp.cos(ang)
    sin_ref[0] = jnp.sin(ang)


def _rope_tables(positions):
    B, S = positions.shape
    T = B * S
    ha, hc = MLA_ROPE // 2, DSA_ROT // 2
    fa = jnp.exp(-math.log(ROPE_THETA) * jnp.arange(ha, dtype=F32) * (2.0 / MLA_ROPE))
    fc = jnp.exp(-math.log(ROPE_THETA) * jnp.arange(hc, dtype=F32) * (2.0 / DSA_ROT))
    freqs = jnp.concatenate([fa, fc])
    nf = ha + hc
    cos, sin = pl.pallas_call(
        _trig_kernel,
        grid=(nf,),
        in_specs=[pl.BlockSpec(memory_space=pltpu.SMEM),
                  pl.BlockSpec((B, S), lambda i: (0, 0))],
        out_specs=[pl.BlockSpec((1, B, S), lambda i: (i, 0, 0)),
                   pl.BlockSpec((1, B, S), lambda i: (i, 0, 0))],
        out_shape=[jax.ShapeDtypeStruct((nf, B, S), F32)] * 2,
        compiler_params=_cparams(("arbitrary",)),
        name="rope_trig",
    )(freqs, positions)
    cos = cos.reshape(nf, T)
    sin = sin.reshape(nf, T)
    ca, sa, cc, sc = cos[:ha], sin[:ha], cos[ha:], sin[ha:]
    cos_at = jnp.concatenate([ca, ca, jnp.ones((HEAD_PAD - 2 * ha, T), F32)], axis=0)
    sin_at = jnp.concatenate([-sa, sa, jnp.zeros((HEAD_PAD - 2 * ha, T), F32)], axis=0)
    cos_ct = jnp.concatenate([cc, cc, jnp.ones((DSA_HEAD_DIM - 2 * hc, T), F32)], axis=0)
    sin_ct = jnp.concatenate([-sc, sc, jnp.zeros((DSA_HEAD_DIM - 2 * hc, T), F32)], axis=0)
    cos_a, sin_a = cos_at.T, sin_at.T
    cos_c = jnp.concatenate([cos_ct, cos_ct], axis=0).T
    sin_c = jnp.concatenate([sin_ct, sin_ct], axis=0).T
    return dict(cos_a=cos_a, sin_a=sin_a, cos_c=cos_c, sin_c=sin_c,
                cos_at=cos_at, sin_at=sin_at, cos_ct=cos_ct, sin_ct=sin_ct)


def _front_kernel(x_ref, sh_ref, sc_ref, g_ref, w_ref, wt_ref, qn_ref, wuq_ref, kvn_ref, wk_ref, wv_ref, cw_ref,
                  ca_ref, sa_ref, cc_ref, sn_ref, cat_ref, sat_ref, cct_ref, snt_ref,
                  q_out, k_out, v_out, ob_out, qc_out, qi_out, kk_out, vc_out, iw_out,
                  carry_ref, *, tiles_per_seq):
    i = pl.program_id(0)
    x = x_ref[...]
    ms = jnp.mean(x * x, axis=-1, keepdims=True)
    h = (x * lax.rsqrt(ms + NORM_EPS) * g_ref[...]) * (1.0 + sc_ref[0]) + sh_ref[0]
    hb = h.astype(BF16)

    def proj(start, width):
        return jnp.dot(hb, w_ref[:, start:start + width], preferred_element_type=F32)

    def proj_t(start, rows):
        return lax.dot_general(wt_ref[start:start + rows, :], hb, NT, preferred_element_type=F32)

    q_lat = proj(SEG_QLAT, MLA_Q_LORA)
    cq = q_lat * lax.rsqrt(jnp.mean(q_lat * q_lat, axis=-1, keepdims=True) + NORM_EPS) * qn_ref[...]
    cqb = cq.astype(BF16)
    kv_lat = proj(SEG_KVLAT, MLA_KV_LORA)
    ckv = kv_lat * lax.rsqrt(jnp.mean(kv_lat * kv_lat, axis=-1, keepdims=True) + NORM_EPS) * kvn_ref[...]
    ckvb = ckv.astype(BF16)
    k_pe = _rope(proj(SEG_KPE, LANES), ca_ref[...], sa_ref[...], MLA_ROPE // 2, LANES)
    cat, sat = cat_ref[...], sat_ref[...]
    q_scale = (MLA_NOPE + MLA_ROPE) ** -0.5 * LOG2E
    for hd in range(MLA_HEADS):
        qh = lax.dot_general(wuq_ref[hd], cqb, NT, preferred_element_type=F32)
        q_out[0, hd] = (_rope_t(qh, cat, sat, MLA_ROPE // 2, HEAD_PAD) * q_scale).astype(BF16)
        kh = jnp.dot(ckvb, wk_ref[hd], preferred_element_type=F32) + k_pe
        k_out[0, hd] = kh.astype(BF16)
        v_out[0, hd] = lax.dot_general(wv_ref[hd], ckvb, NT, preferred_element_type=F32).astype(BF16)

    g_b = proj(SEG_CONV, CONV_DIM)
    u = proj(SEG_CONV + CONV_DIM, CONV_DIM) * proj(SEG_CONV + 2 * CONV_DIM, CONV_DIM)
    tm = u.shape[0]

    @pl.when(i % tiles_per_seq == 0)
    def _():
        carry_ref[...] = jnp.zeros(carry_ref.shape, F32)

    prev = carry_ref[...]
    row = lax.broadcasted_iota(I32, u.shape, 0)
    u1 = jnp.where(row == 0, prev[7:8], pltpu.roll(u, 1, axis=0))
    u2 = jnp.where(row == 0, prev[6:7], jnp.where(row == 1, prev[7:8], pltpu.roll(u, 2, axis=0)))
    carry_ref[...] = u[tm - SUBLANES:tm]
    cw = cw_ref[...]
    y = cw[0:1] * u2 + cw[1:2] * u1 + cw[2:3] * u
    ob_out[...] = (g_b * y).astype(BF16)

    hc = DSA_ROT // 2
    cct, snt = cct_ref[...], snt_ref[...]
    nq_rows = DSA_HEADS * DSA_HEAD_DIM
    qc_scale = DSA_HEAD_DIM ** -0.5 * LOG2E
    qc_out[0] = (_rope_t(proj_t(ROW_DQ, nq_rows), cct, snt, hc, DSA_HEAD_DIM) * qc_scale).astype(BF16)
    qi_out[0] = (_rope_t(proj_t(ROW_IQ, nq_rows), cct, snt, hc, IDX_DIM) * IDX_DIM ** -0.5).astype(BF16)
    kk_out[...] = _rope(proj(SEG_KK, LANES), cc_ref[...], sn_ref[...], hc, DSA_HEAD_DIM).astype(BF16)
    vc_out[0] = proj_t(ROW_VC, DSA_HEAD_DIM).astype(BF16)
    iw_out[0] = proj_t(ROW_IW, IDX_HEADS) * IDX_HEADS ** -0.5


def _front(x2, shift, scale, norm_g, wts, tabs, B, S, tm):
    T, D = x2.shape
    tps = S // tm
    nt = T // tm
    row = lambda i: (i, 0)
    col = lambda i: (0, i)
    bat = lambda i: (i // tps, 0, 0)
    head = lambda i: (i // tps, 0, i % tps, 0)
    head_t = lambda i: (i // tps, 0, 0, i % tps)
    seq_t = lambda i: (i // tps, 0, i % tps)
    full2 = lambda i: (0, 0)
    full3 = lambda i: (0, 0, 0)
    H = MLA_HEADS
    QR = DSA_HEADS * DSA_HEAD_DIM
    in_specs = [
        pl.BlockSpec((tm, D), row),
        pl.BlockSpec((1, 1, D), bat),
        pl.BlockSpec((1, 1, D), bat),
        pl.BlockSpec((1, D), full2),
        pl.BlockSpec((D, MAIN_COLS), full2),
        pl.BlockSpec((T_ROWS, D), full2),
        pl.BlockSpec((1, MLA_Q_LORA), full2),
        pl.BlockSpec((H, HEAD_PAD, MLA_Q_LORA), full3),
        pl.BlockSpec((1, MLA_KV_LORA), full2),
        pl.BlockSpec((H, MLA_KV_LORA, HEAD_PAD), full3),
        pl.BlockSpec((H, MLA_V, MLA_KV_LORA), full3),
        pl.BlockSpec((CONV_WIDTH, CONV_DIM), full2),
        pl.BlockSpec((tm, LANES), row),
        pl.BlockSpec((tm, LANES), row),
        pl.BlockSpec((tm, LANES), row),
        pl.BlockSpec((tm, LANES), row),
        pl.BlockSpec((HEAD_PAD, tm), col),
        pl.BlockSpec((HEAD_PAD, tm), col),
        pl.BlockSpec((DSA_HEAD_DIM, tm), col),
        pl.BlockSpec((DSA_HEAD_DIM, tm), col),
    ]
    out_specs = [
        pl.BlockSpec((1, H, HEAD_PAD, tm), head_t),
        pl.BlockSpec((1, H, tm, HEAD_PAD), head),
        pl.BlockSpec((1, H, MLA_V, tm), head_t),
        pl.BlockSpec((tm, CONV_DIM), row),
        pl.BlockSpec((1, QR, tm), seq_t),
        pl.BlockSpec((1, QR, tm), seq_t),
        pl.BlockSpec((tm, LANES), row),
        pl.BlockSpec((1, DSA_HEAD_DIM, tm), seq_t),
        pl.BlockSpec((1, IDX_HEADS, tm), seq_t),
    ]
    out_shape = [
        jax.ShapeDtypeStruct((B, H, HEAD_PAD, S), BF16),
        jax.ShapeDtypeStruct((B, H, S, HEAD_PAD), BF16),
        jax.ShapeDtypeStruct((B, H, MLA_V, S), BF16),
        jax.ShapeDtypeStruct((T, CONV_DIM), BF16),
        jax.ShapeDtypeStruct((B, QR, S), BF16),
        jax.ShapeDtypeStruct((B, QR, S), BF16),
        jax.ShapeDtypeStruct((T, LANES), BF16),
        jax.ShapeDtypeStruct((B, DSA_HEAD_DIM, S), BF16),
        jax.ShapeDtypeStruct((B, IDX_HEADS, S), F32),
    ]
    return pl.pallas_call(
        functools.partial(_front_kernel, tiles_per_seq=tps),
        grid=(nt,),
        in_specs=in_specs,
        out_specs=out_specs,
        out_shape=out_shape,
        scratch_shapes=[pltpu.VMEM((SUBLANES, CONV_DIM), F32)],
        compiler_params=_cparams(("arbitrary",)),
        name="front",
    )(x2, shift, scale, norm_g, wts["w_main"], wts["w_t"], wts["q_norm"], wts["w_uq"], wts["kv_norm"],
      wts["w_k"], wts["w_v"], wts["conv_w"],
      tabs["cos_a"], tabs["sin_a"], tabs["cos_c"], tabs["sin_c"],
      tabs["cos_at"], tabs["sin_at"], tabs["cos_ct"], tabs["sin_ct"])


def _attend_heads(key_tile, q_tile, v_tile, bias, m_ref, l_ref, acc_ref, nheads, vdim):
    scores = [jnp.dot(key_tile(hd), q_tile(hd), preferred_element_type=F32) for hd in range(nheads)]
    probs, alphas = [], []
    for hd in range(nheads):
        s = scores[hd] if bias is None else scores[hd] + bias
        m_new = jnp.maximum(m_ref[hd], jnp.max(s, axis=0, keepdims=True))
        alpha = jnp.exp2(m_ref[hd] - m_new)
        p = jnp.exp2(s - m_new[0:1])
        l_ref[hd] = alpha * l_ref[hd] + jnp.sum(p, axis=0, keepdims=True)
        m_ref[hd] = m_new
        probs.append(p.astype(BF16))
        alphas.append(alpha[0:1])
    for hd in range(nheads):
        rows = slice(hd * vdim, (hd + 1) * vdim)
        acc_ref[rows, :] = alphas[hd] * acc_ref[rows, :] + jnp.dot(v_tile(hd), probs[hd],
                                                                   preferred_element_type=F32)


def _attend_init(m_ref, l_ref, acc_ref):
    m_ref[...] = jnp.full(m_ref.shape, MASK_BIAS, F32)
    l_ref[...] = jnp.zeros(l_ref.shape, F32)
    acc_ref[...] = jnp.zeros(acc_ref.shape, F32)


def _attend_finish(o_ref, l_ref, acc_ref, nheads, vdim):
    outs = [acc_ref[hd * vdim:(hd + 1) * vdim, :] / l_ref[hd, 0:1, :] for hd in range(nheads)]
    o_ref[...] = jnp.concatenate(outs, axis=0).T.astype(BF16)


def _mla_kernel(q_ref, k_ref, v_ref, o_ref, m_ref, l_ref, acc_ref, *, tq, tk):
    i = pl.program_id(1)
    sub = tq // tk
    _attend_init(m_ref, l_ref, acc_ref)

    def chunk(c, bias):
        off = pl.multiple_of(c * tk, tk)
        _attend_heads(lambda hd: k_ref[0, hd, pl.ds(off, tk), :],
                      lambda hd: q_ref[0, hd],
                      lambda hd: v_ref[0, hd, :, pl.ds(off, tk)],
                      bias, m_ref, l_ref, acc_ref, MLA_HEADS, MLA_V)

    def body(c, _):
        chunk(c, None)
        return 0

    lax.fori_loop(0, i * sub, body, 0)
    kpos = lax.broadcasted_iota(I32, (tk, tq), 0)
    qpos = lax.broadcasted_iota(I32, (tk, tq), 1)
    for j in range(sub):
        chunk(i * sub + j, jnp.where(j * tk + kpos <= qpos, 0.0, MASK_BIAS))
    _attend_finish(o_ref, l_ref, acc_ref, MLA_HEADS, MLA_V)


def _mla(q, k, v, tq, tk):
    B, H, S, _ = k.shape
    nq = S // tq
    return pl.pallas_call(
        functools.partial(_mla_kernel, tq=tq, tk=tk),
        grid=(B, nq),
        in_specs=[pl.BlockSpec((1, H, HEAD_PAD, tq), lambda b, i: (b, 0, 0, i)),
                  pl.BlockSpec((1, H, S, HEAD_PAD), lambda b, i: (b, 0, 0, 0)),
                  pl.BlockSpec((1, H, MLA_V, S), lambda b, i: (b, 0, 0, 0))],
        out_specs=pl.BlockSpec((tq, H * MLA_V), lambda b, i: (b * nq + i, 0)),
        out_shape=jax.ShapeDtypeStruct((B * S, H * MLA_V), BF16),
        scratch_shapes=[pltpu.VMEM((H, SUBLANES, tq), F32), pltpu.VMEM((H, SUBLANES, tq), F32),
                        pltpu.VMEM((H * MLA_V, tq), F32)],
        compiler_params=_cparams(("arbitrary", "arbitrary")),
        name="mla_attention",
    )(q, k, v)


def _float_key(score):
    bits = pltpu.bitcast(score + 0.0, I32)
    return jnp.where(bits >= 0, bits, bits ^ jnp.int32(0x7FFFFFFF))


def _dsa_kernel(qi_ref, kk_ref, iw_ref, qc_ref, vc_ref, o_ref,
                key_ref, cut_ref, m_ref, l_ref, acc_ref, *, tq, tk, topk, seq):
    i = pl.program_id(1)
    nch = (i + 1) * (tq // tk)
    q0 = i * tq
    row = lax.broadcasted_iota(I32, (tk, tq), 0)
    qpos = q0 + lax.broadcasted_iota(I32, (tk, tq), 1)
    HD = DSA_HEAD_DIM

    iw = iw_ref[0]

    def score_chunk(c, _):
        off = pl.multiple_of(c * tk, tk)
        kic = kk_ref[pl.ds(off, tk), :][:, :IDX_DIM]
        score = jnp.zeros((tk, tq), F32)
        for hd in range(IDX_HEADS):
            lg = jnp.dot(kic, qi_ref[0, hd * IDX_DIM:(hd + 1) * IDX_DIM, :], preferred_element_type=F32)
            score = score + jnp.maximum(lg, 0.0) * iw[hd:hd + 1, :]
        key_ref[pl.ds(off, tk), :] = jnp.where(off + row <= qpos, _float_key(score), INT_MIN)
        return 0

    lax.fori_loop(0, nch, score_chunk, 0)

    def count(pred):
        def body(c, part):
            off = pl.multiple_of(c * tk, tk)
            hit = jnp.where(pred(key_ref[pl.ds(off, tk), :], off), 1.0, 0.0)
            return part + jnp.sum(hit.reshape(tk // SUBLANES, SUBLANES, tq), axis=0)
        part = lax.fori_loop(0, nch, body, jnp.zeros((SUBLANES, tq), F32))
        return jnp.sum(part, axis=0, keepdims=True)

    def search(it, carry):
        t, n_ge = carry
        cand = t + (jnp.int32(1) << (31 - it))
        cnt = count(lambda kc, off: kc >= cand)
        ok = cnt >= topk
        return jnp.where(ok, cand, t), jnp.where(ok, cnt, n_ge)

    thr, n_ge = lax.fori_loop(0, 32, search,
                              (jnp.full((1, tq), INT_MIN, I32), jnp.full((1, tq), float(topk), F32)))

    tie = (thr > INT_MIN) & (n_ge > topk)
    cut_ref[...] = jnp.full(cut_ref.shape, seq, I32)

    @pl.when(jnp.max(jnp.where(tie, 1.0, 0.0)) > 0.0)
    def _():
        need = topk - count(lambda kc, off: kc > thr)
        nbits = max(1, (seq - 1).bit_length())

        def find(it, p):
            cand = p + (jnp.int32(1) << (nbits - 1 - it))
            cnt = count(lambda kc, off: (kc == thr) & (off + row < cand))
            return jnp.where(cnt < need, cand, p)

        cut = lax.fori_loop(0, nbits, find, jnp.zeros((1, tq), I32))
        cut_ref[...] = jnp.broadcast_to(jnp.where(tie, cut, seq), cut_ref.shape)

    _attend_init(m_ref, l_ref, acc_ref)
    cut1 = cut_ref[0:1, :]

    def attend(c, _):
        off = pl.multiple_of(c * tk, tk)
        key = key_ref[pl.ds(off, tk), :]
        kpos = off + row
        sel = ((key > thr) | ((key == thr) & (kpos <= cut1))) & (kpos <= qpos)
        bias = jnp.where(sel, 0.0, MASK_BIAS)
        kc = kk_ref[pl.ds(off, tk), :][:, IDX_DIM:]
        vt = vc_ref[0, :, pl.ds(off, tk)]
        _attend_heads(lambda hd: kc, lambda hd: qc_ref[0, hd * HD:(hd + 1) * HD, :], lambda hd: vt,
                      bias, m_ref, l_ref, acc_ref, DSA_HEADS, HD)
        return 0

    lax.fori_loop(0, nch, attend, 0)
    _attend_finish(o_ref, l_ref, acc_ref, DSA_HEADS, HD)


def _dsa(qi, kk, iw, qc, vc, tq, tk):
    B, QR, S = qc.shape
    nq = S // tq
    topk = min(TOPK_MAX, S // 4)
    return pl.pallas_call(
        functools.partial(_dsa_kernel, tq=tq, tk=tk, topk=topk, seq=S),
        grid=(B, nq),
        in_specs=[pl.BlockSpec((1, QR, tq), lambda b, i: (b, 0, i)),
                  pl.BlockSpec((S, LANES), lambda b, i: (b, 0)),
                  pl.BlockSpec((1, IDX_HEADS, tq), lambda b, i: (b, 0, i)),
                  pl.BlockSpec((1, QR, tq), lambda b, i: (b, 0, i)),
                  pl.BlockSpec((1, DSA_HEAD_DIM, S), lambda b, i: (b, 0, 0))],
        out_specs=pl.BlockSpec((tq, QR), lambda b, i: (b * nq + i, 0)),
        out_shape=jax.ShapeDtypeStruct((B * S, QR), BF16),
        scratch_shapes=[pltpu.VMEM((S, tq), I32),
                        pltpu.VMEM((SUBLANES, tq), I32),
                        pltpu.VMEM((DSA_HEADS, SUBLANES, tq), F32),
                        pltpu.VMEM((DSA_HEADS, SUBLANES, tq), F32),
                        pltpu.VMEM((QR, tq), F32)],
        compiler_params=_cparams(("arbitrary", "arbitrary")),
        name="dsa_attention",
    )(qi, kk, iw, qc, vc)


def _merge_kernel(x_ref, sh1_ref, sc1_ref, g1_ref, sh2_ref, sc2_ref, n1_ref, n2_ref,
                  oa_ref, ob_ref, oc_ref, wg_ref, bg_ref, wb_ref, wo_ref, rw_ref, rb_ref,
                  xo_ref, h2_ref, idx_ref, gate_ref):
    x = x_ref[...]
    tm, D = x.shape
    ms = jnp.mean(x * x, axis=-1, keepdims=True)
    h = (x * lax.rsqrt(ms + NORM_EPS) * n1_ref[...]) * (1.0 + sc1_ref[0]) + sh1_ref[0]
    hb = h.astype(BF16)
    mix = jnp.zeros(x.shape, F32)
    for n, o_ref in enumerate((oa_ref, ob_ref, oc_ref)):
        y = jnp.dot(o_ref[...], wb_ref[n], preferred_element_type=F32)
        gl = jnp.dot(hb, wg_ref[:, n * D:(n + 1) * D], preferred_element_type=F32) + bg_ref[:, n * D:(n + 1) * D]
        mix = mix + jax.nn.sigmoid(gl) * y
    out = jnp.dot(mix.astype(BF16), wo_ref[...], preferred_element_type=F32)
    xn = x + g1_ref[0] * out
    xo_ref[...] = xn
    ms2 = jnp.mean(xn * xn, axis=-1, keepdims=True)
    h2 = (xn * lax.rsqrt(ms2 + NORM_EPS) * n2_ref[...]) * (1.0 + sc2_ref[0]) + sh2_ref[0]
    for s in range(D // LANES):
        h2_ref[pl.ds(s, tm, stride=D // LANES), :] = h2[:, s * LANES:(s + 1) * LANES]
    logits = lax.dot_general(rw_ref[...], h2.astype(BF16), NT, preferred_element_type=F32) + rb_ref[...]
    eidx = lax.broadcasted_iota(I32, logits.shape, 0)
    vals, idxs = [], []
    for _ in range(TOP_K):
        mx = jnp.max(logits, axis=0, keepdims=True)
        am = jnp.min(jnp.where(logits == mx, eidx, N_EXPERTS), axis=0, keepdims=True)
        vals.append(mx)
        idxs.append(am)
        logits = jnp.where(eidx == am, -jnp.inf, logits)
    ex = [jnp.exp(v - vals[0]) for v in vals]
    den = ex[0] + ex[1] + ex[2] + ex[3]
    pad_i = [jnp.zeros_like(idxs[0])] * (SUBLANES - TOP_K)
    pad_f = [jnp.zeros_like(den)] * (SUBLANES - TOP_K)
    idx_ref[...] = jnp.concatenate(idxs + pad_i, axis=0)
    gate_ref[...] = jnp.concatenate([e / den for e in ex] + pad_f, axis=0)


def _merge(x2, mods, n1, n2, oa, ob, oc, wts, B, S, tm):
    T, D = x2.shape
    tps = S // tm
    R = D // LANES
    row = lambda i: (i, 0)
    bat = lambda i: (i // tps, 0, 0)
    full2 = lambda i: (0, 0)
    full3 = lambda i: (0, 0, 0)
    mod_spec = pl.BlockSpec((1, 1, D), bat)
    W = BRANCH_WIDTH
    return pl.pallas_call(
        _merge_kernel,
        grid=(T // tm,),
        in_specs=[pl.BlockSpec((tm, D), row)] + [mod_spec] * 5 + [
            pl.BlockSpec((1, D), full2), pl.BlockSpec((1, D), full2),
            pl.BlockSpec((tm, W), row), pl.BlockSpec((tm, W), row), pl.BlockSpec((tm, W), row),
            pl.BlockSpec((D, N_BRANCH * D), full2), pl.BlockSpec((1, N_BRANCH * D), full2),
            pl.BlockSpec((N_BRANCH, W, D), full3), pl.BlockSpec((D, D), full2),
            pl.BlockSpec((N_EXPERTS, D), full2), pl.BlockSpec((N_EXPERTS, 1), full2)],
        out_specs=[pl.BlockSpec((tm, D), row), pl.BlockSpec((tm * R, LANES), row),
                   pl.BlockSpec((SUBLANES, tm), lambda i: (0, i)), pl.BlockSpec((SUBLANES, tm), lambda i: (0, i))],
        out_shape=[jax.ShapeDtypeStruct((T, D), F32), jax.ShapeDtypeStruct((T * R, LANES), F32),
                   jax.ShapeDtypeStruct((SUBLANES, T), I32), jax.ShapeDtypeStruct((SUBLANES, T), F32)],
        compiler_params=_cparams(("arbitrary",)),
        name="merge_router",
    )(x2, *mods, n1, n2, oa, ob, oc, wts["w_gate"], wts["b_gate"], wts["w_branch"], wts["w_out"],
      wts["router_wt"], wts["router_b"])


def _deint_kernel(w_ref, p_ref, o_ref):
    o_ref[0] = jnp.dot(w_ref[0, 0].astype(BF16), p_ref[...], preferred_element_type=F32).astype(BF16)


def _deinterleave_w1(exp_w1, l):
    _, E, D, F2 = exp_w1.shape
    half = DEINT // 2
    j = jnp.arange(half)
    perm = jnp.zeros((DEINT, DEINT), BF16).at[2 * j, j].set(1.0).at[2 * j + 1, half + j].set(1.0)
    return pl.pallas_call(
        _deint_kernel,
        grid=(E, F2 // DEINT),
        in_specs=[pl.BlockSpec((1, 1, D, DEINT), lambda e, b: (l, e, 0, b)),
                  pl.BlockSpec((DEINT, DEINT), lambda e, b: (0, 0))],
        out_specs=pl.BlockSpec((1, D, DEINT), lambda e, b: (e, 0, b)),
        out_shape=jax.ShapeDtypeStruct((E, D, F2), BF16),
        compiler_params=_cparams(("arbitrary", "arbitrary")),
        name="w1_deinterleave",
    )(exp_w1, perm)


def _moe_kernel(be_ref, nu_ref, tok0_ref, tok1_ref, tokn_ref, dstp_ref, x_hbm, w1_ref, w2_ref, b1_ref, b2_ref,
                y_hbm, xbuf, ybuf, gsem, ssem, *, tb, hchunk, n_real):
    i = pl.program_id(0)
    nused = nu_ref[0]
    slot = i % 2
    other = 1 - slot
    gslot = i % 3
    R = SUBLANES

    def gather_copy(tok_ref, s, r):
        src = pl.multiple_of(tok_ref[0, 0, r], R)
        return pltpu.make_async_copy(x_hbm.at[pl.ds(src, R)], xbuf.at[s, pl.ds(r * R, R)], gsem.at[s])

    def scatter_copy(s, r):
        dst = pl.multiple_of(dstp_ref[0, 0, r], R)
        return pltpu.make_async_copy(ybuf.at[s, pl.ds(r * R, R)], y_hbm.at[pl.ds(dst, R)], ssem.at[s])

    def wait_gather(s):
        pltpu.make_async_copy(x_hbm.at[pl.ds(0, tb * R)], xbuf.at[s], gsem.at[s]).wait()

    def wait_scatter(s):
        pltpu.make_async_copy(ybuf.at[s], y_hbm.at[pl.ds(0, tb * R)], ssem.at[s]).wait()

    @pl.when(i == 0)
    def _():
        ybuf[...] = jnp.zeros(ybuf.shape, F32)
        for s in range(2):
            fill = pltpu.make_async_copy(ybuf.at[s], y_hbm.at[pl.ds((n_real + s * tb) * R, tb * R)], ssem.at[s])
            fill.start()
            fill.wait()

        def body(r, _):
            for s, tok_ref in enumerate((tok0_ref, tok1_ref)):
                src = pl.multiple_of(tok_ref[0, 0, r], R)
                pltpu.make_async_copy(x_hbm.at[pl.ds(src, R)], xbuf.at[s, pl.ds(pl.multiple_of(r * R, R), R)],
                                      gsem.at[s]).start()
            return 0
        lax.fori_loop(0, tb, body, 0, unroll=8)

    @pl.when(i < nused)
    def _():
        wait_gather(gslot)
        D = R * LANES
        xb = jnp.concatenate([xbuf[gslot, pl.ds(s, tb, stride=R), :] for s in range(R)], axis=1).astype(BF16)
        ahead = (i + 2) % 3
        for r in range(tb):
            scatter_copy(other, r).start(priority=r % 2)
            gather_copy(tokn_ref, ahead, r).start(priority=(r + 1) % 2)
        y = jnp.zeros((tb, D), F32) + b2_ref[0]
        q = DEINT // 2
        for j in range(D // hchunk):
            cs = slice(2 * j * hchunk, 2 * (j + 1) * hchunk)
            hgu = jnp.dot(xb, w1_ref[0, :, cs], preferred_element_type=F32) + b1_ref[0, :, cs]
            nb = 2 * hchunk // DEINT
            hg = jnp.concatenate([hgu[:, b * DEINT:b * DEINT + q] for b in range(nb)], axis=1)
            hu = jnp.concatenate([hgu[:, b * DEINT + q:(b + 1) * DEINT] for b in range(nb)], axis=1)
            gate = jnp.minimum(hg, SWIGLU_LIMIT)
            up = jnp.clip(hu, -SWIGLU_LIMIT, SWIGLU_LIMIT)
            act = (up + 1.0) * (gate * jax.nn.sigmoid(gate * SWIGLU_ALPHA))
            y = y + jnp.dot(act.astype(BF16), w2_ref[0, j * hchunk:(j + 1) * hchunk, :],
                            preferred_element_type=F32)
        @pl.when(i >= 1)
        def _():
            wait_scatter(slot)

        for s in range(R):
            ybuf[slot, pl.ds(s, tb, stride=R), :] = y[:, s * LANES:(s + 1) * LANES]

    @pl.when(i == nused)
    def _():
        wait_scatter(slot)
        wait_gather(gslot)
        wait_gather((i + 1) % 3)

        def body(r, _):
            dst = pl.multiple_of(dstp_ref[0, 0, r], R)
            pltpu.make_async_copy(ybuf.at[other, pl.ds(pl.multiple_of(r * R, R), R)], y_hbm.at[pl.ds(dst, R)],
                                  ssem.at[other]).start()
            return 0
        lax.fori_loop(0, tb, body, 0, unroll=8)
        wait_scatter(other)


def _moe(h2t, route, wts, tb, n_real):
    R = SUBLANES
    D = R * LANES
    block_expert, nused, tok_blk, dst_shift = route
    nblk = tok_blk.shape[0]
    tok3 = tok_blk.reshape(nblk, 1, tb)
    dst3 = dst_shift.reshape(nblk + 1, 1, tb)
    smem_blk = lambda f: pl.BlockSpec((1, 1, tb), f, memory_space=pltpu.SMEM)
    F2 = wts["w1"].shape[2]
    blk = lambda i, be: be[jnp.minimum(i, nblk - 1)]
    grid_spec = pltpu.PrefetchScalarGridSpec(
        num_scalar_prefetch=2,
        grid=(nblk + 1,),
        in_specs=[smem_blk(lambda i, be, nu: (0, 0, 0)),
                  smem_blk(lambda i, be, nu: (min(1, nblk - 1), 0, 0)),
                  smem_blk(lambda i, be, nu: (jnp.minimum(i + 2, nblk - 1), 0, 0)),
                  smem_blk(lambda i, be, nu: (i, 0, 0)),
                  pl.BlockSpec(memory_space=pl.ANY),
                  pl.BlockSpec((1, D, F2), lambda i, be, nu: (blk(i, be), 0, 0)),
                  pl.BlockSpec((1, F2 // 2, D), lambda i, be, nu: (blk(i, be), 0, 0)),
                  pl.BlockSpec((1, 1, F2), lambda i, be, nu: (blk(i, be), 0, 0)),
                  pl.BlockSpec((1, 1, D), lambda i, be, nu: (blk(i, be), 0, 0))],
        out_specs=pl.BlockSpec(memory_space=pl.ANY),
        scratch_shapes=[pltpu.VMEM((3, tb * R, LANES), F32), pltpu.VMEM((2, tb * R, LANES), F32),
                        pltpu.SemaphoreType.DMA((3,)), pltpu.SemaphoreType.DMA((2,))],
    )
    return pl.pallas_call(
        functools.partial(_moe_kernel, tb=tb, hchunk=256, n_real=n_real),
        grid_spec=grid_spec,
        out_shape=jax.ShapeDtypeStruct(((n_real + 2 * tb) * R, LANES), F32),
        compiler_params=_cparams(("arbitrary",)),
        name="moe_ffn",
    )(block_expert, nused, tok3, tok3, tok3, dst3, h2t, wts["w1"], wts["w2"], wts["b1"], wts["b2"])


def _route(top_idx, tb):
    E = N_EXPERTS
    R = SUBLANES
    T = top_idx.shape[0]
    N = T * TOP_K
    assert E * N < 2 ** 31
    nblk = N // tb + E
    keys = jnp.sort(top_idx.reshape(N) * N + jnp.arange(N, dtype=I32))
    order = keys % N
    starts = jnp.searchsorted(keys, jnp.arange(E + 1, dtype=I32) * N).astype(I32)
    counts = starts[1:] - starts[:-1]
    nb = (counts + tb - 1) // tb
    bends = jnp.cumsum(nb)
    nused = bends[E - 1]
    blk = jnp.arange(nblk, dtype=I32)
    be = jnp.minimum(jnp.searchsorted(bends, jnp.minimum(blk, nused - 1), side='right'), E - 1).astype(I32)
    j = blk - (bends - nb)[be]
    first = starts[be] + j * tb
    length = jnp.where(blk < nused, jnp.clip(counts[be] - j * tb, 0, tb), 0)
    r = jnp.arange(tb, dtype=I32)
    slot_blk = order[jnp.clip(first[:, None] + r[None, :], 0, N - 1)]
    tok_blk = (slot_blk // TOP_K) * R
    dump = N + (blk % 2)[:, None] * tb + r[None, :]
    row_blk = (slot_blk % TOP_K) * T + slot_blk // TOP_K
    dst_blk = jnp.where(r[None, :] < length[:, None], row_blk, dump) * R
    dst_shift = jnp.concatenate([(N + tb + r[None, :]) * R, dst_blk], axis=0)
    return be, nused.reshape(1).astype(I32), tok_blk, dst_shift


def _combine_kernel(x_ref, g2_ref, gate_ref, y0_ref, y1_ref, y2_ref, y3_ref, fg_ref, o_ref, *, final):
    g = gate_ref[...]
    tm, D = x_ref.shape
    R = D // LANES
    cols = []
    for s in range(R):
        rows = pl.ds(s, tm, stride=R)
        cols.append(g[:, 0:1] * y0_ref[rows, :] + g[:, 1:2] * y1_ref[rows, :]
                    + g[:, 2:3] * y2_ref[rows, :] + g[:, 3:4] * y3_ref[rows, :])
    x = x_ref[...] + g2_ref[0] * jnp.concatenate(cols, axis=1)
    if final:
        x = x * lax.rsqrt(jnp.mean(x * x, axis=-1, keepdims=True) + NORM_EPS) * fg_ref[...]
    o_ref[...] = x


def _combine(x2, g2, gates, y4, final_g, B, S, tm, final):
    T, D = x2.shape
    R = D // LANES
    tps = S // tm
    nt = T // tm
    row = lambda i: (i, 0)
    yspec = lambda k: pl.BlockSpec((tm * R, LANES), lambda i, k=k: (k * nt + i, 0))
    return pl.pallas_call(
        functools.partial(_combine_kernel, final=final),
        grid=(nt,),
        in_specs=[pl.BlockSpec((tm, D), row),
                  pl.BlockSpec((1, 1, D), lambda i: (i // tps, 0, 0)),
                  pl.BlockSpec((tm, TOP_K), row),
                  yspec(0), yspec(1), yspec(2), yspec(3),
                  pl.BlockSpec((1, D), lambda i: (0, 0))],
        out_specs=pl.BlockSpec((tm, D), row),
        out_shape=jax.ShapeDtypeStruct((T, D), F32),
        compiler_params=_cparams(("arbitrary",)),
        name="combine",
    )(x2, g2, gates, y4, y4, y4, y4, final_g)


def _pack_layer(l, w_in, b_gate, mla_q_norm, mla_w_uq, mla_kv_norm, mla_w_ukv, conv_w, w_branch, w_out,
                router_w, router_b, exp_w1, exp_b1, exp_w2, exp_b2):
    D = w_in.shape[1]
    wi = w_in[l]
    o = [0]

    def seg(n):
        s = wi[:, o[0]:o[0] + n]
        o[0] += n
        return s
    q_lat, kv_full, conv = seg(MLA_Q_LORA), seg(MLA_KV_LORA + MLA_ROPE), seg(3 * CONV_DIM)
    dq, dkv, iq = seg(DSA_HEADS * DSA_HEAD_DIM), seg(2 * DSA_HEAD_DIM), seg(IDX_HEADS * IDX_DIM)
    ik, iw, gates = seg(IDX_DIM), seg(IDX_HEADS), seg(N_BRANCH * D)
    zeros = lambda n: jnp.zeros((D, n), wi.dtype)
    w_main = jnp.concatenate([
        q_lat, kv_full[:, :MLA_KV_LORA],
        kv_full[:, MLA_KV_LORA:], zeros(LANES - MLA_ROPE),
        ik, dkv[:, :DSA_HEAD_DIM],
        conv], axis=1).astype(BF16)
    assert w_main.shape[1] == MAIN_COLS
    w_t = jnp.concatenate([dq, iq, dkv[:, DSA_HEAD_DIM:], iw, zeros(T_ROWS - ROW_IW - IDX_HEADS)],
                          axis=1).T.astype(BF16)
    assert w_t.shape[0] == T_ROWS
    H = MLA_HEADS
    hq = MLA_ROPE + MLA_NOPE
    w_uq = mla_w_uq[l].reshape(MLA_Q_LORA, H, hq)
    w_uq = jnp.pad(w_uq, ((0, 0), (0, 0), (0, HEAD_PAD - hq))).transpose(1, 2, 0).astype(BF16)
    w_ukv = mla_w_ukv[l].reshape(MLA_KV_LORA, H, MLA_NOPE + MLA_V)
    w_k = jnp.pad(w_ukv[:, :, :MLA_NOPE], ((0, 0), (0, 0), (MLA_ROPE, HEAD_PAD - hq)))
    w_k = w_k.transpose(1, 0, 2).astype(BF16)
    w_v = w_ukv[:, :, MLA_NOPE:].transpose(1, 2, 0).astype(BF16)
    b1 = exp_b1[l]
    E, F2 = b1.shape
    b1p = b1.reshape(E, F2 // DEINT, DEINT // 2, 2).transpose(0, 1, 3, 2).reshape(E, 1, F2)
    return dict(
        w_main=w_main, w_t=w_t,
        q_norm=mla_q_norm[l].reshape(1, -1), kv_norm=mla_kv_norm[l].reshape(1, -1),
        w_uq=w_uq, w_k=w_k, w_v=w_v, conv_w=conv_w[l],
        w_gate=gates.astype(BF16), b_gate=b_gate[l].reshape(1, -1),
        w_branch=w_branch[l].astype(BF16), w_out=w_out[l].astype(BF16),
        router_wt=router_w[l].T.astype(BF16), router_b=router_b[l].reshape(-1, 1),
        w1=_deinterleave_w1(exp_w1, l), w2=exp_w2[l].astype(BF16),
        b1=b1p, b2=exp_b2[l].reshape(E, 1, -1),
    )


def _tiles(S, T):
    tm = min(512, S)
    tq = min(256, S)
    tk = min(256, tq)
    N = T * TOP_K
    tb = 512 if N >= 512 * N_EXPERTS else 128
    return tm, tq, tk, tb


def kernel(x, c, positions, norm1_g, norm2_g, w_ada, b_ada, w_in, b_gate, mla_q_norm, mla_w_uq, mla_kv_norm,
           mla_w_ukv, conv_w, w_branch, w_out, router_w, router_b, exp_w1, exp_b1, exp_w2, exp_b2, final_g):
    B, S, D = x.shape
    T = B * S
    L = w_ada.shape[0]
    tm, tq, tk, tb = _tiles(S, T)
    mod = _ada(c, w_ada, b_ada)
    tabs = _rope_tables(positions)
    x2 = x.reshape(T, D)
    for l in range(L):
        wts = _pack_layer(l, w_in, b_gate, mla_q_norm, mla_w_uq, mla_kv_norm, mla_w_ukv, conv_w, w_branch,
                          w_out, router_w, router_b, exp_w1, exp_b1, exp_w2, exp_b2)
        sh1, sc1, g1, sh2, sc2, g2 = [m.reshape(B, 1, D) for m in jnp.split(mod[l], 6, axis=-1)]
        n1 = norm1_g[l].reshape(1, D)
        n2 = norm2_g[l].reshape(1, D)
        q, k, v, ob, qc, qi, kk, vc, iw = _front(x2, sh1, sc1, n1, wts, tabs, B, S, tm)
        oa = _mla(q, k, v, min(2 * tq, S), tk)
        oc = _dsa(qi, kk, iw, qc, vc, tq, tk)
        x2, h2t, idx8, gate8 = _merge(x2, (sh1, sc1, g1, sh2, sc2), n1, n2, oa, ob, oc, wts, B, S, tm)
        y4 = _moe(h2t, _route(idx8[:TOP_K].T, tb), wts, tb, T * TOP_K)
        x2 = _combine(x2, g2, gate8[:TOP_K].T, y4, final_g.reshape(1, D), B, S, tm, final=(l == L - 1))
    return x2.reshape(B, S, D)
```
